```python
import math, functools
import jax, jax.numpy as jnp
from jax import lax
import numpy as np

D_MODEL = 1024
BATCH = 2
SEQ = 16384
DEPTH = 4
DEC_BATCH = 8
DEC_SEQ = 32
PAST_LEN = 1024

CHUNK = 64
N_MIXERS = 3
EXPAND = 2
E_WIDTH = EXPAND * D_MODEL
N_HEADS = 16
HEAD_DIM = E_WIDTH // N_HEADS
DIFF_DIM = HEAD_DIM // 2
BAND_CHUNKS = 8
BAND_ROWS = BAND_CHUNKS * CHUNK
REL_CLIP = 128
QBLK = 128
EPS = 1e-6
SUBLN_EPS = 1e-5
N_A = (DEPTH + 2) // 3
N_B = (DEPTH + 1) // 3
N_C = DEPTH // 3

kernel_name = 'hybrid_stream_diff_band_stickbreak'


def rmsnorm(x, g, eps=EPS):
    xf = x.astype(jnp.float32)
    y = xf * lax.rsqrt(jnp.mean(xf * xf, axis=-1, keepdims=True) + eps)
    return (y * g.astype(jnp.float32)).astype(x.dtype)


def alibi_slopes(n):
    return 2.0 ** (-8.0 * jnp.arange(1, n + 1, dtype=jnp.float32) / n)


def diff_attn_block(q, k, v, q_pos, k_pos, lam, lam_init, subln_g):
    f32 = jnp.float32
    qf = q.astype(f32) * (DIFF_DIM ** -0.5)
    kf = k.astype(f32)
    s1 = jnp.einsum('bqhd,bkhd->bhqk', qf[..., :DIFF_DIM], kf[..., :DIFF_DIM])
    s2 = jnp.einsum('bqhd,bkhd->bhqk', qf[..., DIFF_DIM:], kf[..., DIFF_DIM:])
    dist = jnp.abs(q_pos[:, None] - k_pos[None, :]).astype(f32)
    allowed = (k_pos[None, :] // CHUNK) <= (q_pos[:, None] // CHUNK)
    bias = jnp.where(allowed[None], -alibi_slopes(N_HEADS)[:, None, None] * dist[None], -jnp.inf)
    p = jax.nn.softmax(s1 + bias, axis=-1) - lam * jax.nn.softmax(s2 + bias, axis=-1)
    o = jnp.einsum('bhqk,bkhe->bqhe', p, v.astype(f32))
    o = rmsnorm(o, subln_g, SUBLN_EPS) * (1.0 - lam_init)
    return o.astype(q.dtype)


def band_attn_block(q, k, v, q_pos, k_pos, rel_bias):
    f32 = jnp.float32
    s = jnp.einsum('bqhd,bkhd->bhqk', q.astype(f32) * (HEAD_DIM ** -0.5), k.astype(f32))
    rel = q_pos[:, None] - k_pos[None, :]
    idx = jnp.clip(rel, -REL_CLIP, REL_CLIP) + REL_CLIP
    bias = rel_bias.astype(f32)[:, idx]
    qc = q_pos[:, None] // CHUNK
    kc = k_pos[None, :] // CHUNK
    valid = (k_pos[None, :] >= 0) & (kc <= qc) & (kc >= qc - BAND_CHUNKS)
    p = jax.nn.softmax(jnp.where(valid[None, None], s + bias[None], -jnp.inf), axis=-1)
    o = jnp.einsum('bhqk,bkhe->bqhe', p, v.astype(f32))
    return o.astype(q.dtype)


def stick_block(q, k, v, q_pos, k_pos):
    f32 = jnp.float32
    z = jnp.einsum('bqhd,bkhd->bhqk', q.astype(f32) * (HEAD_DIM ** -0.5), k.astype(f32))
    valid = (k_pos[None, :] < q_pos[:, None])[None, None]
    log_beta = jax.nn.log_sigmoid(z)
    log_keep = jnp.where(valid, jax.nn.log_sigmoid(-z), 0.0)
    after = lax.cumsum(log_keep, axis=3, reverse=True) - log_keep
    w = jnp.where(valid, jnp.exp(log_beta + after), 0.0)
    o = jnp.einsum('bhqk,bkhe->bqhe', w, v.astype(f32))
    return o.astype(q.dtype)


def causal_sweep(block_fn, q, k, v, pos):
    T = q.shape[1]
    outs = [block_fn(q[:, s:s + QBLK], k[:, :s + QBLK], v[:, :s + QBLK], pos[s:s + QBLK], pos[:s + QBLK])
            for s in range(0, T, QBLK)]
    return jnp.concatenate(outs, axis=1)


def band_prompt(q, k, v, rel_bias):
    T = q.shape[1]
    pad = ((0, 0), (BAND_ROWS, 0), (0, 0), (0, 0))
    kpad = jnp.pad(k, pad)
    vpad = jnp.pad(v, pad)

    def one_chunk(c):
        start = c * CHUNK
        qc = lax.dynamic_slice_in_dim(q, start, CHUNK, axis=1)
        kc = lax.dynamic_slice_in_dim(kpad, start, BAND_ROWS + CHUNK, axis=1)
        vc = lax.dynamic_slice_in_dim(vpad, start, BAND_ROWS + CHUNK, axis=1)
        q_pos = start + jnp.arange(CHUNK, dtype=jnp.int32)
        k_pos = start - BAND_ROWS + jnp.arange(BAND_ROWS + CHUNK, dtype=jnp.int32)
        return band_attn_block(qc, kc, vc, q_pos, k_pos, rel_bias)

    out = lax.map(one_chunk, jnp.arange(T // CHUNK, dtype=jnp.int32))
    return jnp.moveaxis(out, 0, 1).reshape(q.shape[0], T, N_HEADS, HEAD_DIM)


def in_proj(x, g, w):
    h = rmsnorm(x, g) @ w
    q, k, v, gate = jnp.split(h, 4, axis=-1)
    shp = (x.shape[0], x.shape[1], N_HEADS, HEAD_DIM)
    return q.reshape(shp), k.reshape(shp), v.reshape(shp), gate


def out_proj(o, gate, w):
    o = o.reshape(o.shape[0], o.shape[1], E_WIDTH)
    return (o * jax.nn.silu(gate)) @ w


def setup_inputs(seed: int = 0) -> dict:
    key = jax.random.key(seed)
    ks = jax.random.split(key, 16)
    nrm = jax.random.normal
    f32 = jnp.float32
    b_keep = min(BAND_ROWS, PAST_LEN)
    return {
        'x_prompt': nrm(ks[0], (BATCH, SEQ, D_MODEL), f32),
        'x_sample': nrm(ks[1], (DEC_BATCH, DEC_SEQ, D_MODEL), f32),
        'cache_a_k': nrm(ks[2], (N_A, DEC_BATCH, PAST_LEN, N_HEADS, HEAD_DIM), f32),
        'cache_a_v': nrm(ks[3], (N_A, DEC_BATCH, PAST_LEN, N_HEADS, HEAD_DIM), f32),
        'cache_b_k': nrm(ks[4], (N_B, DEC_BATCH, b_keep, N_HEADS, HEAD_DIM), f32),
        'cache_b_v': nrm(ks[5], (N_B, DEC_BATCH, b_keep, N_HEADS, HEAD_DIM), f32),
        'cache_c_k': nrm(ks[6], (N_C, DEC_BATCH, PAST_LEN, N_HEADS, HEAD_DIM), f32),
        'cache_c_v': nrm(ks[7], (N_C, DEC_BATCH, PAST_LEN, N_HEADS, HEAD_DIM), f32),
        'norm_g': 1.0 + 0.05 * nrm(ks[8], (DEPTH, D_MODEL), f32),
        'w_in': nrm(ks[9], (DEPTH, D_MODEL, 4 * E_WIDTH), f32) * (D_MODEL ** -0.5),
        'w_out': nrm(ks[10], (DEPTH, E_WIDTH, D_MODEL), f32) * (E_WIDTH ** -0.5),
        'a_lambda': 0.1 * nrm(ks[11], (N_A, 4, DIFF_DIM), f32),
        'a_subln_g': 1.0 + 0.05 * nrm(ks[12], (N_A, HEAD_DIM), f32),
        'b_rel_bias': 0.1 * nrm(ks[13], (N_B, N_HEADS, 2 * REL_CLIP + 1), f32),
        'final_g': 1.0 + 0.05 * nrm(ks[14], (D_MODEL,), f32),
    }


def reference(x_prompt, x_sample, cache_a_k, cache_a_v, cache_b_k, cache_b_v, cache_c_k, cache_c_v,
              norm_g, w_in, w_out, a_lambda, a_subln_g, b_rel_bias, final_g):
    T = x_prompt.shape[1]
    n = x_sample.shape[1]
    past_len = cache_a_k.shape[2]
    b_keep = cache_b_k.shape[2]
    pos_p = jnp.arange(T, dtype=jnp.int32)
    pos_s = past_len + jnp.arange(n, dtype=jnp.int32)
    pos_ctx = jnp.arange(past_len + n, dtype=jnp.int32)
    pos_band = past_len - b_keep + jnp.arange(b_keep + n, dtype=jnp.int32)
    keep_p = min(BAND_ROWS, T)

    xp, xs = x_prompt, x_sample
    a_k_p, a_v_p, b_k_p, b_v_p, c_k_p, c_v_p = [], [], [], [], [], []
    a_k_s, a_v_s, b_k_s, b_v_s, c_k_s, c_v_s = [], [], [], [], [], []
    for i in range(DEPTH):
        kind = i % N_MIXERS
        j = i // N_MIXERS
        qp, kp, vp, gp = in_proj(xp, norm_g[i], w_in[i])
        qs, ks_, vs, gs = in_proj(xs, norm_g[i], w_in[i])
        if kind == 0:
            lam_init = 0.8 - 0.6 * math.exp(-0.3 * i)
            lp = a_lambda[j].astype(jnp.float32)
            lam = jnp.exp(jnp.sum(lp[0] * lp[1])) - jnp.exp(jnp.sum(lp[2] * lp[3])) + lam_init
            fn = functools.partial(diff_attn_block, lam=lam, lam_init=lam_init, subln_g=a_subln_g[j])
            op = causal_sweep(fn, qp, kp, vp, pos_p)
            k_ctx = jnp.concatenate([cache_a_k[j], ks_], axis=1)
            v_ctx = jnp.concatenate([cache_a_v[j], vs], axis=1)
            os_ = fn(qs, k_ctx, v_ctx, pos_s, pos_ctx)
            a_k_p.append(kp)
            a_v_p.append(vp)
            a_k_s.append(ks_)
            a_v_s.append(vs)
        elif kind == 1:
            op = band_prompt(qp, kp, vp, b_rel_bias[j])
            k_ctx = jnp.concatenate([cache_b_k[j], ks_], axis=1)
            v_ctx = jnp.concatenate([cache_b_v[j], vs], axis=1)
            os_ = band_attn_block(qs, k_ctx, v_ctx, pos_s, pos_band, b_rel_bias[j])
            b_k_p.append(kp[:, T - keep_p:])
            b_v_p.append(vp[:, T - keep_p:])
            b_k_s.append(k_ctx[:, n:])
            b_v_s.append(v_ctx[:, n:])
        else:
            op = causal_sweep(stick_block, qp, kp, vp, pos_p)
            k_ctx = jnp.concatenate([cache_c_k[j], ks_], axis=1)
            v_ctx = jnp.concatenate([cache_c_v[j], vs], axis=1)
            os_ = stick_block(qs, k_ctx, v_ctx, pos_s, pos_ctx)
            c_k_p.append(kp)
            c_v_p.append(vp)
            c_k_s.append(ks_)
            c_v_s.append(vs)
        xp = xp + out_proj(op, gp, w_out[i])
        xs = xs + out_proj(os_, gs, w_out[i])

    y_prompt = rmsnorm(xp, final_g)
    y_sample = rmsnorm(xs, final_g)
    return (y_prompt, y_sample,
            jnp.stack(a_k_p), jnp.stack(a_v_p), jnp.stack(b_k_p), jnp.stack(b_v_p),
            jnp.stack(c_k_p), jnp.stack(c_v_p),
            jnp.stack(a_k_s), jnp.stack(a_v_s), jnp.stack(b_k_s), jnp.stack(b_v_s),
            jnp.stack(c_k_s), jnp.stack(c_v_s))
```

```python
import functools
import math

import jax
import jax.numpy as jnp
from jax import lax
from jax.experimental import pallas as pl
from jax.experimental.pallas import tpu as pltpu

F32 = jnp.float32
BF16 = jnp.bfloat16

CHUNK = 64
N_MIXERS = 3
N_HEADS = 16
HEAD_DIM = 128
DIFF_DIM = HEAD_DIM // 2
BAND_CHUNKS = 8
BAND_ROWS = BAND_CHUNKS * CHUNK
REL_CLIP = 128
EPS = 1e-6
SUBLN_EPS = 1e-5

VMEM_LIMIT_BYTES = 56 * 1024 * 1024
NEG_BIG = -1e30
STICK_DEAD = -105.0

IN_PROJ_ROWS = 256
OUT_PROJ_ROWS = 512
DIFF_BLOCK = 256
BAND_QROWS = 128
STICK_BLOCK = 128


def _params(*sem):
    return pltpu.CompilerParams(dimension_semantics=sem, vmem_limit_bytes=VMEM_LIMIT_BYTES)


def _dot(a, b):
    return jnp.dot(a, b, preferred_element_type=F32)


def _dot_nt(a, b):
    return lax.dot_general(a, b, (((1,), (1,)), ((), ())), preferred_element_type=F32)


def _iota2(shape, dim):
    return lax.broadcasted_iota(jnp.int32, shape, dim)


def _in_proj_kernel(x_ref, g_ref, w_ref, q_ref, k_ref, v_ref, kb_ref, vb_ref, sg_ref, *, qscale, width):
    x = x_ref[...]
    ms = jnp.mean(x * x, axis=-1, keepdims=True)
    xn = (x * lax.rsqrt(ms + EPS) * g_ref[...]).astype(BF16)
    q = _dot(xn, w_ref[:, 0:width])
    q_ref[...] = (q * qscale).astype(BF16)
    k = _dot(xn, w_ref[:, width:2 * width])
    k_ref[...] = k
    kb_ref[...] = k.astype(BF16)
    v = _dot(xn, w_ref[:, 2 * width:3 * width])
    v_ref[...] = v
    vb_ref[...] = v.astype(BF16)
    gate = _dot(xn, w_ref[:, 3 * width:4 * width])
    sg_ref[...] = (gate * jax.nn.sigmoid(gate)).astype(BF16)


def _in_proj(x2d, g, w_bf, qscale):
    m, d = x2d.shape
    width = w_bf.shape[1] // 4
    tm = min(IN_PROJ_ROWS, m)
    row = lambda i: (i, 0)
    fixed = lambda i: (0, 0)
    out_block = pl.BlockSpec((tm, width), row)
    return pl.pallas_call(
        functools.partial(_in_proj_kernel, qscale=qscale, width=width),
        grid=(m // tm,),
        in_specs=[pl.BlockSpec((tm, d), row),
                  pl.BlockSpec((1, d), fixed),
                  pl.BlockSpec((d, 4 * width), fixed, pipeline_mode=pl.Buffered(1))],
        out_specs=[out_block] * 6,
        out_shape=[jax.ShapeDtypeStruct((m, width), BF16),
                   jax.ShapeDtypeStruct((m, width), F32),
                   jax.ShapeDtypeStruct((m, width), F32),
                   jax.ShapeDtypeStruct((m, width), BF16),
                   jax.ShapeDtypeStruct((m, width), BF16),
                   jax.ShapeDtypeStruct((m, width), BF16)],
        compiler_params=_params("arbitrary"),
        name="in_proj",
    )(x2d, g.reshape(1, d), w_bf)


def _out_proj_kernel(o_ref, sg_ref, w_ref, x_ref, fg_ref, y_ref, *, final):
    a = o_ref[...] * sg_ref[...]
    y = x_ref[...] + _dot(a, w_ref[...])
    if final:
        ms = jnp.mean(y * y, axis=-1, keepdims=True)
        y = y * lax.rsqrt(ms + EPS) * fg_ref[...]
    y_ref[...] = y


def _out_proj(o2d, sg2d, w_bf, x2d, final_g, final):
    m, e = o2d.shape
    d = x2d.shape[1]
    tm = min(OUT_PROJ_ROWS, m)
    row = lambda i: (i, 0)
    fixed = lambda i: (0, 0)
    return pl.pallas_call(
        functools.partial(_out_proj_kernel, final=final),
        grid=(m // tm,),
        in_specs=[pl.BlockSpec((tm, e), row),
                  pl.BlockSpec((tm, e), row),
                  pl.BlockSpec((e, d), fixed),
                  pl.BlockSpec((tm, d), row),
                  pl.BlockSpec((1, d), fixed)],
        out_specs=pl.BlockSpec((tm, d), row),
        out_shape=jax.ShapeDtypeStruct((m, d), F32),
        compiler_params=_params("arbitrary"),
        name="out_proj_final" if final else "out_proj",
    )(o2d, sg2d, w_bf, x2d, final_g.reshape(1, d))


def _split_q(q):
    lane = _iota2(q.shape, 1)
    zero = jnp.zeros_like(q)
    return jnp.where(lane < DIFF_DIM, q, zero), jnp.where(lane >= DIFF_DIM, q, zero)


def _diff_lambda(lam_ref, lam_init):
    lp = lam_ref[0]
    e1 = jnp.exp(jnp.sum(lp[0:1, :] * lp[1:2, :], axis=1, keepdims=True))
    e2 = jnp.exp(jnp.sum(lp[2:3, :] * lp[3:4, :], axis=1, keepdims=True))
    return e1 - e2 + lam_init


def _diff_finish(o1, l1, o2, l2, lam, g, lam_init):
    o = o1 / l1 - lam * (o2 / l2)
    ms = jnp.mean(o * o, axis=-1, keepdims=True)
    return o * lax.rsqrt(ms + SUBLN_EPS) * g * (1.0 - lam_init)


def _online_step(s, m, l, acc, v):
    m_new = jnp.maximum(m, jnp.max(s, axis=-1, keepdims=True))
    alpha = jnp.exp(m - m_new)
    p = jnp.exp(s - m_new)
    l_new = alpha * l + jnp.sum(p, axis=-1, keepdims=True)
    acc_new = alpha * acc + _dot(p.astype(BF16), v)
    return m_new, l_new, acc_new


def _diff_prompt_kernel(slope_ref, lam_ref, g_ref, q_ref, k_ref, v_ref, o_ref, *, blk, lam_init):
    h = pl.program_id(1)
    qi = pl.program_id(2)
    slope = slope_ref[h]
    q1, q2 = _split_q(q_ref[0])
    rel = (_iota2((blk, blk), 0) - _iota2((blk, blk), 1)).astype(F32)
    boff = -slope * rel

    def body(j, carry):
        m1, l1, a1, m2, l2, a2 = carry
        start = pl.multiple_of(j * blk, blk)
        k = k_ref[0, pl.ds(start, blk), :]
        v = v_ref[0, pl.ds(start, blk), :]
        bias = boff - slope * ((qi - j) * blk).astype(F32)
        m1, l1, a1 = _online_step(_dot_nt(q1, k) + bias, m1, l1, a1, v)
        m2, l2, a2 = _online_step(_dot_nt(q2, k) + bias, m2, l2, a2, v)
        return m1, l1, a1, m2, l2, a2

    m0 = jnp.full((blk, 1), NEG_BIG, F32)
    l0 = jnp.zeros((blk, 1), F32)
    a0 = jnp.zeros((blk, HEAD_DIM), F32)
    m1, l1, a1, m2, l2, a2 = lax.fori_loop(0, qi, body, (m0, l0, a0, m0, l0, a0))

    start = pl.multiple_of(qi * blk, blk)
    k = k_ref[0, pl.ds(start, blk), :]
    v = v_ref[0, pl.ds(start, blk), :]
    allowed = (_iota2((blk, blk), 1) // CHUNK) <= (_iota2((blk, blk), 0) // CHUNK)
    bias = jnp.where(allowed, -slope * jnp.abs(rel), NEG_BIG)
    m1, l1, a1 = _online_step(_dot_nt(q1, k) + bias, m1, l1, a1, v)
    m2, l2, a2 = _online_step(_dot_nt(q2, k) + bias, m2, l2, a2, v)

    lam = _diff_lambda(lam_ref, lam_init)
    o_ref[0] = _diff_finish(a1, l1, a2, l2, lam, g_ref[...], lam_init).astype(o_ref.dtype)


def _diff_prompt(q, k, v, slopes, lam_p, subln_g, lam_init):
    b, t, e = q.shape
    blk = min(DIFF_BLOCK, t)
    head_rows = lambda bi, hi, qi: (bi, 0, hi)
    q_rows = lambda bi, hi, qi: (bi, qi, hi)
    return pl.pallas_call(
        functools.partial(_diff_prompt_kernel, blk=blk, lam_init=lam_init),
        grid=(b, N_HEADS, t // blk),
        in_specs=[pl.BlockSpec(memory_space=pltpu.SMEM),
                  pl.BlockSpec((1, 4, DIFF_DIM), lambda bi, hi, qi: (0, 0, 0)),
                  pl.BlockSpec((1, HEAD_DIM), lambda bi, hi, qi: (0, 0)),
                  pl.BlockSpec((1, blk, HEAD_DIM), q_rows),
                  pl.BlockSpec((1, t, HEAD_DIM), head_rows),
                  pl.BlockSpec((1, t, HEAD_DIM), head_rows)],
        out_specs=pl.BlockSpec((1, blk, HEAD_DIM), q_rows),
        out_shape=jax.ShapeDtypeStruct((b, t, e), BF16),
        compiler_params=_params("arbitrary", "arbitrary", "arbitrary"),
        name="diff_prompt",
    )(slopes, lam_p.reshape(1, 4, DIFF_DIM), subln_g.reshape(1, HEAD_DIM), q, k, v)


def _softmax_parts(scores, values):
    m = functools.reduce(jnp.maximum, [jnp.max(s, axis=-1, keepdims=True) for s in scores])
    ps = [jnp.exp(s - m) for s in scores]
    l = sum(jnp.sum(p, axis=-1, keepdims=True) for p in ps)
    o = sum(_dot(p.astype(BF16), v) for p, v in zip(ps, values))
    return o, l


def _diff_sample_kernel(slope_ref, lam_ref, g_ref, q_ref, kn_ref, vn_ref, kc_ref, vc_ref, o_ref, *,
                        past, lam_init):
    h = pl.program_id(1)
    slope = slope_ref[h]
    q1, q2 = _split_q(q_ref[0])
    n = q1.shape[0]
    kc = kc_ref[0, 0].astype(BF16)
    vc = vc_ref[0, 0].astype(BF16)
    kn = kn_ref[0]
    vn = vn_ref[0]

    def bias(k_start, k_len):
        q_pos = past + _iota2((n, k_len), 0)
        k_pos = k_start + _iota2((n, k_len), 1)
        allowed = (k_pos // CHUNK) <= (q_pos // CHUNK)
        return jnp.where(allowed, -slope * jnp.abs(q_pos - k_pos).astype(F32), NEG_BIG)

    bias_c = bias(0, kc.shape[0])
    bias_n = bias(past, n)
    o1, l1 = _softmax_parts([_dot_nt(q1, kc) + bias_c, _dot_nt(q1, kn) + bias_n], [vc, vn])
    o2, l2 = _softmax_parts([_dot_nt(q2, kc) + bias_c, _dot_nt(q2, kn) + bias_n], [vc, vn])
    lam = _diff_lambda(lam_ref, lam_init)
    o_ref[0] = _diff_finish(o1, l1, o2, l2, lam, g_ref[...], lam_init).astype(o_ref.dtype)


def _sample_specs(n, past, layer):
    new_rows = lambda bi, hi: (bi, 0, hi)
    cache_rows = lambda bi, hi: (layer, bi, 0, hi)
    new = pl.BlockSpec((1, n, HEAD_DIM), new_rows)
    cache = pl.BlockSpec((1, 1, past, HEAD_DIM), cache_rows)
    return new, cache


def _diff_sample(q, k_new, v_new, cache_k, cache_v, layer, slopes, lam_p, subln_g, lam_init):
    b, n, e = q.shape
    past = cache_k.shape[2]
    new, cache = _sample_specs(n, past, layer)
    return pl.pallas_call(
        functools.partial(_diff_sample_kernel, past=past, lam_init=lam_init),
        grid=(b, N_HEADS),
        in_specs=[pl.BlockSpec(memory_space=pltpu.SMEM),
                  pl.BlockSpec((1, 4, DIFF_DIM), lambda bi, hi: (0, 0, 0)),
                  pl.BlockSpec((1, HEAD_DIM), lambda bi, hi: (0, 0)),
                  new, new, new, cache, cache],
        out_specs=new,
        out_shape=jax.ShapeDtypeStruct((b, n, e), BF16),
        compiler_params=_params("arbitrary", "arbitrary"),
        name="diff_sample",
    )(slopes, lam_p.reshape(1, 4, DIFF_DIM), subln_g.reshape(1, HEAD_DIM), q, k_new, v_new, cache_k, cache_v)


def _band_prompt_kernel(bias_ref, q_ref, k_ref, v_ref, o_ref, *, tq, span):
    qi = pl.program_id(2)
    start = pl.multiple_of(qi * tq, tq)
    k = k_ref[0, pl.ds(start, span), :]
    v = v_ref[0, pl.ds(start, span), :]
    q_pos = qi * tq + _iota2((tq, span), 0)
    k_pos = qi * tq - BAND_ROWS + _iota2((tq, span), 1)
    qc = q_pos // CHUNK
    kc = (k_pos + BAND_ROWS) // CHUNK - BAND_CHUNKS
    valid = (k_pos >= 0) & (kc <= qc) & (kc >= qc - BAND_CHUNKS)
    s = jnp.where(valid, _dot_nt(q_ref[0], k) + bias_ref[0], NEG_BIG)
    o, l = _softmax_parts([s], [v])
    o_ref[0] = (o / l).astype(o_ref.dtype)


def _band_prompt(q, k_pad, v_pad, bias):
    b, t, e = q.shape
    tq = bias.shape[1]
    span = bias.shape[2]
    head_rows = lambda bi, hi, qi: (bi, 0, hi)
    q_rows = lambda bi, hi, qi: (bi, qi, hi)
    return pl.pallas_call(
        functools.partial(_band_prompt_kernel, tq=tq, span=span),
        grid=(b, N_HEADS, t // tq),
        in_specs=[pl.BlockSpec((1, tq, span), lambda bi, hi, qi: (hi, 0, 0)),
                  pl.BlockSpec((1, tq, HEAD_DIM), q_rows),
                  pl.BlockSpec((1, t + BAND_ROWS, HEAD_DIM), head_rows),
                  pl.BlockSpec((1, t + BAND_ROWS, HEAD_DIM), head_rows)],
        out_specs=pl.BlockSpec((1, tq, HEAD_DIM), q_rows),
        out_shape=jax.ShapeDtypeStruct((b, t, e), BF16),
        compiler_params=_params("arbitrary", "arbitrary", "arbitrary"),
        name="band_prompt",
    )(bias, q, k_pad, v_pad)


def _band_sample_kernel(bc_ref, bn_ref, q_ref, kn_ref, vn_ref, kc_ref, vc_ref, o_ref, *, past):
    q = q_ref[0]
    n = q.shape[0]
    kc = kc_ref[0, 0].astype(BF16)
    vc = vc_ref[0, 0].astype(BF16)
    keep = kc.shape[0]

    def valid(k_start, k_len):
        q_pos = past + _iota2((n, k_len), 0)
        k_pos = k_start + _iota2((n, k_len), 1)
        qc = q_pos // CHUNK
        kc_ = k_pos // CHUNK
        return (k_pos >= 0) & (kc_ <= qc) & (kc_ >= qc - BAND_CHUNKS)

    s_c = jnp.where(valid(past - keep, keep), _dot_nt(q, kc) + bc_ref[0], NEG_BIG)
    s_n = jnp.where(valid(past, n), _dot_nt(q, kn_ref[0]) + bn_ref[0], NEG_BIG)
    o, l = _softmax_parts([s_c, s_n], [vc, vn_ref[0]])
    o_ref[0] = (o / l).astype(o_ref.dtype)


def _band_sample(q, k_new, v_new, cache_k, cache_v, layer, bias_c, bias_n, past):
    b, n, e = q.shape
    keep = cache_k.shape[2]
    new, cache = _sample_specs(n, keep, layer)
    return pl.pallas_call(
        functools.partial(_band_sample_kernel, past=past),
        grid=(b, N_HEADS),
        in_specs=[pl.BlockSpec((1, n, keep), lambda bi, hi: (hi, 0, 0)),
                  pl.BlockSpec((1, n, n), lambda bi, hi: (hi, 0, 0)),
                  new, new, new, cache, cache],
        out_specs=new,
        out_shape=jax.ShapeDtypeStruct((b, n, e), BF16),
        compiler_params=_params("arbitrary", "arbitrary"),
        name="band_sample",
    )(bias_c, bias_n, q, k_new, v_new, cache_k, cache_v)


def _band_bias(rel_bias, q_pos, k_pos):
    idx = jnp.clip(q_pos[:, None] - k_pos[None, :], -REL_CLIP, REL_CLIP) + REL_CLIP
    return rel_bias.astype(F32)[:, idx]


def _strict_upper(nk):
    return jnp.where(_iota2((nk, nk), 0) > _iota2((nk, nk), 1), 1.0, 0.0).astype(BF16)


def _stick_block(q, k, v, upper, carry, valid):
    z = _dot_nt(q, k)
    soft = jnp.log1p(jnp.exp(-jnp.abs(z)))
    log_beta = jnp.minimum(z, 0.0) - soft
    log_keep = -jnp.maximum(z, 0.0) - soft
    if valid is not None:
        log_keep = jnp.where(valid, log_keep, 0.0)
    hi = log_keep.astype(BF16)
    lo = (log_keep - hi.astype(F32)).astype(BF16)
    after = _dot(hi, upper) + _dot(lo, upper) + carry
    w = jnp.exp(log_beta + after)
    if valid is not None:
        w = jnp.where(valid, w, 0.0)
    out = _dot(w.astype(BF16), v)
    return out, carry + jnp.sum(log_keep, axis=-1, keepdims=True)


def _stick_prompt_kernel(q_ref, k_ref, v_ref, o_ref, *, blk):
    qi = pl.program_id(2)
    q = q_ref[0]
    upper = _strict_upper(blk)

    start = pl.multiple_of(qi * blk, blk)
    causal = _iota2((blk, blk), 1) < _iota2((blk, blk), 0)
    acc, carry = _stick_block(q, k_ref[0, pl.ds(start, blk), :], v_ref[0, pl.ds(start, blk), :], upper,
                              jnp.zeros((blk, 1), F32), causal)

    def cond(state):
        j, alive, _, _ = state
        return (j >= 0) & (alive > STICK_DEAD)

    def body(state):
        j, _, acc, carry = state
        start = pl.multiple_of(j * blk, blk)
        out, carry = _stick_block(q, k_ref[0, pl.ds(start, blk), :], v_ref[0, pl.ds(start, blk), :], upper,
                                  carry, None)
        return j - 1, jnp.max(carry), acc + out, carry

    _, _, acc, _ = lax.while_loop(cond, body, (qi - 1, jnp.max(carry), acc, carry))
    o_ref[0] = acc.astype(o_ref.dtype)


def _stick_prompt(q, k, v):
    b, t, e = q.shape
    blk = min(STICK_BLOCK, t)
    head_rows = lambda bi, hi, qi: (bi, 0, hi)
    q_rows = lambda bi, hi, qi: (bi, qi, hi)
    return pl.pallas_call(
        functools.partial(_stick_prompt_kernel, blk=blk),
        grid=(b, N_HEADS, t // blk),
        in_specs=[pl.BlockSpec((1, blk, HEAD_DIM), q_rows),
                  pl.BlockSpec((1, t, HEAD_DIM), head_rows),
                  pl.BlockSpec((1, t, HEAD_DIM), head_rows)],
        out_specs=pl.BlockSpec((1, blk, HEAD_DIM), q_rows),
        out_shape=jax.ShapeDtypeStruct((b, t, e), BF16),
        compiler_params=_params("arbitrary", "arbitrary", "arbitrary"),
        name="stick_prompt",
    )(q, k, v)


def _stick_sample_kernel(q_ref, kn_ref, vn_ref, kc_ref, vc_ref, o_ref, *, blk):
    q = q_ref[0]
    n = q.shape[0]
    past = kc_ref.shape[2]
    upper = _strict_upper(blk)
    causal = _iota2((n, n), 1) < _iota2((n, n), 0)
    acc, carry = _stick_block(q, kn_ref[0], vn_ref[0], upper[:n, :n], jnp.zeros((n, 1), F32), causal)
    for j in reversed(range(past // blk)):
        kc = kc_ref[0, 0, j * blk:(j + 1) * blk, :].astype(BF16)
        vc = vc_ref[0, 0, j * blk:(j + 1) * blk, :].astype(BF16)
        out, carry = _stick_block(q, kc, vc, upper, carry, None)
        acc = acc + out
    o_ref[0] = acc.astype(o_ref.dtype)


def _stick_sample(q, k_new, v_new, cache_k, cache_v, layer):
    b, n, e = q.shape
    past = cache_k.shape[2]
    blk = min(256, past)
    assert past % blk == 0
    new, cache = _sample_specs(n, past, layer)
    return pl.pallas_call(
        functools.partial(_stick_sample_kernel, blk=blk),
        grid=(b, N_HEADS),
        in_specs=[new, new, new, cache, cache],
        out_specs=new,
        out_shape=jax.ShapeDtypeStruct((b, n, e), BF16),
        compiler_params=_params("arbitrary", "arbitrary"),
        name="stick_sample",
    )(q, k_new, v_new, cache_k, cache_v)


def kernel(x_prompt, x_sample, cache_a_k, cache_a_v, cache_b_k, cache_b_v, cache_c_k, cache_c_v,
           norm_g, w_in, w_out, a_lambda, a_subln_g, b_rel_bias, final_g):
    bp, t, d = x_prompt.shape
    bs, n, _ = x_sample.shape
    depth = w_in.shape[0]
    e = w_out.shape[1]
    past = cache_a_k.shape[2]
    keep_s = cache_b_k.shape[2]
    keep_p = min(BAND_ROWS, t)
    heads5 = lambda a, b_: a.reshape(b_, -1, N_HEADS, HEAD_DIM)
    flat_cache = lambda c: c.reshape(c.shape[0], c.shape[1], c.shape[2], e)

    slopes = 2.0 ** (-8.0 * jnp.arange(1, N_HEADS + 1, dtype=F32) / N_HEADS)
    w_in_bf = w_in.astype(BF16)
    w_out_bf = w_out.astype(BF16)
    caches = [(flat_cache(cache_a_k), flat_cache(cache_a_v)),
              (flat_cache(cache_b_k), flat_cache(cache_b_v)),
              (flat_cache(cache_c_k), flat_cache(cache_c_v))]

    xp = x_prompt.reshape(bp * t, d)
    xs = x_sample.reshape(bs * n, d)
    new_p = [([], []) for _ in range(N_MIXERS)]
    new_s = [([], []) for _ in range(N_MIXERS)]
    for i in range(depth):
        kind = i % N_MIXERS
        j = i // N_MIXERS
        last = i == depth - 1
        qscale = DIFF_DIM ** -0.5 if kind == 0 else HEAD_DIM ** -0.5
        qp, kp, vp, kpb, vpb, sgp = _in_proj(xp, norm_g[i], w_in_bf[i], qscale)
        qs, ks, vs, ksb, vsb, sgs = _in_proj(xs, norm_g[i], w_in_bf[i], qscale)
        seq_p = lambda a: a.reshape(bp, t, e)
        seq_s = lambda a: a.reshape(bs, n, e)
        cache_k, cache_v = caches[kind]
        if kind == 0:
            lam_init = 0.8 - 0.6 * math.exp(-0.3 * i)
            op = _diff_prompt(seq_p(qp), seq_p(kpb), seq_p(vpb), slopes, a_lambda[j], a_subln_g[j], lam_init)
            os_ = _diff_sample(seq_s(qs), seq_s(ksb), seq_s(vsb), cache_k, cache_v, j, slopes,
                               a_lambda[j], a_subln_g[j], lam_init)
            new_p[0][0].append(heads5(kp, bp))
            new_p[0][1].append(heads5(vp, bp))
            new_s[0][0].append(heads5(ks, bs))
            new_s[0][1].append(heads5(vs, bs))
        elif kind == 1:
            tq = min(BAND_QROWS, t)
            span = BAND_ROWS + tq
            bias_p = _band_bias(b_rel_bias[j], jnp.arange(tq, dtype=jnp.int32),
                                jnp.arange(span, dtype=jnp.int32) - BAND_ROWS)
            pad = ((0, 0), (BAND_ROWS, 0), (0, 0))
            op = _band_prompt(seq_p(qp), jnp.pad(seq_p(kpb), pad), jnp.pad(seq_p(vpb), pad), bias_p)
            pos_s = past + jnp.arange(n, dtype=jnp.int32)
            bias_c = _band_bias(b_rel_bias[j], pos_s, past - keep_s + jnp.arange(keep_s, dtype=jnp.int32))
            bias_n = _band_bias(b_rel_bias[j], pos_s, pos_s)
            os_ = _band_sample(seq_s(qs), seq_s(ksb), seq_s(vsb), cache_k, cache_v, j, bias_c, bias_n, past)
            new_p[1][0].append(heads5(kp, bp)[:, t - keep_p:])
            new_p[1][1].append(heads5(vp, bp)[:, t - keep_p:])
            new_s[1][0].append(jnp.concatenate([cache_b_k[j], heads5(ks, bs)], axis=1)[:, n:])
            new_s[1][1].append(jnp.concatenate([cache_b_v[j], heads5(vs, bs)], axis=1)[:, n:])
        else:
            op = _stick_prompt(seq_p(qp), seq_p(kpb), seq_p(vpb))
            os_ = _stick_sample(seq_s(qs), seq_s(ksb), seq_s(vsb), cache_k, cache_v, j)
            new_p[2][0].append(heads5(kp, bp))
            new_p[2][1].append(heads5(vp, bp))
            new_s[2][0].append(heads5(ks, bs))
            new_s[2][1].append(heads5(vs, bs))
        xp = _out_proj(op.reshape(bp * t, e), sgp, w_out_bf[i], xp, final_g, last)
        xs = _out_proj(os_.reshape(bs * n, e), sgs, w_out_bf[i], xs, final_g, last)

    outs = [xp.reshape(bp, t, d), xs.reshape(bs, n, d)]
    for group in (new_p, new_s):
        for ks_list, vs_list in group:
            outs.append(jnp.stack(ks_list))
            outs.append(jnp.stack(vs_list))
    return tuple(outs)
```

```python
import functools
import math

import jax
import jax.numpy as jnp
from jax import lax
from jax.experimental import pallas as pl
from jax.experimental.pallas import tpu as pltpu

F32 = jnp.float32
BF16 = jnp.bfloat16

CHUNK = 64
N_MIXERS = 3
N_HEADS = 16
HEAD_DIM = 128
DIFF_DIM = HEAD_DIM // 2
BAND_CHUNKS = 8
BAND_ROWS = BAND_CHUNKS * CHUNK
REL_CLIP = 128
EPS = 1e-6
SUBLN_EPS = 1e-5

VMEM_LIMIT_BYTES = 56 * 1024 * 1024
NEG_BIG = -1e30
STICK_DEAD = -105.0

IN_PROJ_ROWS = 256
OUT_PROJ_ROWS = 512
DIFF_QROWS = 1024
DIFF_KROWS = 256
ONES_ROWS = 16
DIFF_DEAD = -160.0
LOG2E = math.log2(math.e)
BAND_QROWS = 128
STICK_BLOCK = 128


def _params(*sem):
    return pltpu.CompilerParams(dimension_semantics=sem, vmem_limit_bytes=VMEM_LIMIT_BYTES)


def _dot(a, b):
    return jnp.dot(a, b, preferred_element_type=F32)


def _dot_nt(a, b):
    return lax.dot_general(a, b, (((1,), (1,)), ((), ())), preferred_element_type=F32)


def _iota2(shape, dim):
    return lax.broadcasted_iota(jnp.int32, shape, dim)


def _in_proj_kernel(x_ref, g_ref, w_ref, q_ref, k_ref, v_ref, kb_ref, vb_ref, sg_ref, *, qscale, width):
    x = x_ref[...]
    ms = jnp.mean(x * x, axis=-1, keepdims=True)
    xn = (x * lax.rsqrt(ms + EPS) * g_ref[...]).astype(BF16)
    q = _dot(xn, w_ref[:, 0:width])
    q_ref[...] = (q * qscale).astype(BF16)
    k = _dot(xn, w_ref[:, width:2 * width])
    k_ref[...] = k
    kb_ref[...] = k.astype(BF16)
    v = _dot(xn, w_ref[:, 2 * width:3 * width])
    v_ref[...] = v
    vb_ref[...] = v.astype(BF16)
    gate = _dot(xn, w_ref[:, 3 * width:4 * width])
    sg_ref[...] = (gate * jax.nn.sigmoid(gate)).astype(BF16)


def _in_proj(x2d, g, w_bf, qscale):
    m, d = x2d.shape
    width = w_bf.shape[1] // 4
    tm = min(IN_PROJ_ROWS, m)
    row = lambda i: (i, 0)
    fixed = lambda i: (0, 0)
    out_block = pl.BlockSpec((tm, width), row)
    return pl.pallas_call(
        functools.partial(_in_proj_kernel, qscale=qscale, width=width),
        grid=(m // tm,),
        in_specs=[pl.BlockSpec((tm, d), row),
                  pl.BlockSpec((1, d), fixed),
                  pl.BlockSpec((d, 4 * width), fixed, pipeline_mode=pl.Buffered(1))],
        out_specs=[out_block] * 6,
        out_shape=[jax.ShapeDtypeStruct((m, width), BF16),
                   jax.ShapeDtypeStruct((m, width), F32),
                   jax.ShapeDtypeStruct((m, width), F32),
                   jax.ShapeDtypeStruct((m, width), BF16),
                   jax.ShapeDtypeStruct((m, width), BF16),
                   jax.ShapeDtypeStruct((m, width), BF16)],
        compiler_params=_params("arbitrary"),
        name="in_proj",
    )(x2d, g.reshape(1, d), w_bf)


def _out_proj_kernel(o_ref, sg_ref, w_ref, x_ref, fg_ref, y_ref, *, final):
    a = o_ref[...] * sg_ref[...]
    y = x_ref[...] + _dot(a, w_ref[...])
    if final:
        ms = jnp.mean(y * y, axis=-1, keepdims=True)
        y = y * lax.rsqrt(ms + EPS) * fg_ref[...]
    y_ref[...] = y


def _out_proj(o2d, sg2d, w_bf, x2d, final_g, final):
    m, e = o2d.shape
    d = x2d.shape[1]
    tm = min(OUT_PROJ_ROWS, m)
    row = lambda i: (i, 0)
    fixed = lambda i: (0, 0)
    return pl.pallas_call(
        functools.partial(_out_proj_kernel, final=final),
        grid=(m // tm,),
        in_specs=[pl.BlockSpec((tm, e), row),
                  pl.BlockSpec((tm, e), row),
                  pl.BlockSpec((e, d), fixed),
                  pl.BlockSpec((tm, d), row),
                  pl.BlockSpec((1, d), fixed)],
        out_specs=pl.BlockSpec((tm, d), row),
        out_shape=jax.ShapeDtypeStruct((m, d), F32),
        compiler_params=_params("arbitrary"),
        name="out_proj_final" if final else "out_proj",
    )(o2d, sg2d, w_bf, x2d, final_g.reshape(1, d))


def _split_q(q):
    lane = _iota2(q.shape, 1)
    zero = jnp.zeros_like(q)
    return jnp.where(lane < DIFF_DIM, q, zero), jnp.where(lane >= DIFF_DIM, q, zero)


def _diff_lambda(lam_ref, lam_init):
    lp = lam_ref[0]
    e1 = jnp.exp(jnp.sum(lp[0:1, :] * lp[1:2, :], axis=1, keepdims=True))
    e2 = jnp.exp(jnp.sum(lp[2:3, :] * lp[3:4, :], axis=1, keepdims=True))
    return e1 - e2 + lam_init


def _diff_finish(o1, l1, o2, l2, lam, g, lam_init):
    o = o1 / l1 - lam * (o2 / l2)
    ms = jnp.mean(o * o, axis=-1, keepdims=True)
    return o * lax.rsqrt(ms + SUBLN_EPS) * g * (1.0 - lam_init)


def _diff_prompt_kernel(slope_ref, lam_ref, g_ref, qt_ref, k_ref, vt_ref, o_ref, acc_ref, kn_ref, *,
                        tq, tk, lam_init):
    h = pl.program_id(1)
    qb = pl.program_id(2)
    slope2 = slope_ref[h]
    t_keys = k_ref.shape[1]
    lanes = 2 * tq
    nsub = tq // tk
    step = slope2 * tk

    @pl.when(qb == 0)
    def _():
        def body(i, mx):
            kb = jnp.abs(k_ref[0, pl.ds(pl.multiple_of(i * tk, tk), tk), :].astype(F32))
            return jnp.maximum(mx, jnp.max(kb, axis=0, keepdims=True))
        mx = lax.fori_loop(0, t_keys // tk, body, jnp.zeros((1, HEAD_DIM), F32))
        lane = _iota2((1, HEAD_DIM), 1)
        sq = mx * mx
        kn1 = jnp.sqrt(jnp.sum(jnp.where(lane < DIFF_DIM, sq, 0.0), axis=1, keepdims=True))
        kn2 = jnp.sqrt(jnp.sum(jnp.where(lane >= DIFF_DIM, sq, 0.0), axis=1, keepdims=True))
        kn_ref[...] = jnp.where(lane < DIFF_DIM, kn1, kn2)

    qt = qt_ref[0, 0]
    row = _iota2(qt.shape, 0)
    zero = jnp.zeros_like(qt)
    q2t = jnp.concatenate([jnp.where(row < DIFF_DIM, qt, zero), jnp.where(row >= DIFF_DIM, qt, zero)], axis=1)
    qf = q2t.astype(F32)
    qn = jnp.sqrt(jnp.sum(qf * qf, axis=0, keepdims=True))
    kn = jnp.where(_iota2((1, lanes), 1) < tq, kn_ref[0:1, 0:1], kn_ref[0:1, DIFF_DIM:DIFF_DIM + 1])
    bound = qn * kn * 1.001 + 1.0 + slope2 * (tk - 1)

    acc_ref[...] = jnp.zeros_like(acc_ref)

    def visit(k_start, bias, m):
        kblk = k_ref[0, pl.ds(k_start, tk), :]
        vtblk = vt_ref[0, 0, :, pl.ds(k_start, tk)]
        st = _dot(kblk, q2t) + bias
        m_new = jnp.maximum(m, jnp.max(st, axis=0, keepdims=True))
        alpha = jnp.exp2(m - m_new)
        p = jnp.exp2(st - m_new).astype(BF16)
        acc_ref[...] = alpha * acc_ref[...] + _dot(vtblk, p)
        return m_new

    m = jnp.full((1, lanes), NEG_BIG, F32)
    q0 = pl.multiple_of(qb * tq, tq)
    for sub in reversed(range(nsub)):
        jl = _iota2((tk, tq), 0)
        iq = _iota2((tk, tq), 1)
        allowed = ((sub * tk + jl) // CHUNK) <= (iq // CHUNK)
        rel = (iq - sub * tk - jl).astype(F32)
        bias = jnp.where(allowed, slope2 * ((rel - jnp.abs(rel)) + jl.astype(F32)), 2.0 * NEG_BIG)
        m = visit(q0 + sub * tk, jnp.concatenate([bias, bias], axis=1), m + step)

    in_block = slope2 * _iota2((tk, lanes), 0).astype(F32)

    def cond(state):
        j, worst, _ = state
        return (j >= 0) & (worst > DIFF_DEAD)

    def body(state):
        j, _, m = state
        m = visit(pl.multiple_of(j * tk, tk), in_block, m + step)
        return j - 1, jnp.max(bound - (m + step)), m

    lax.while_loop(cond, body, (qb * nsub - 1, jnp.max(bound - (m + step)), m))

    acc = acc_ref[...]
    norm = acc[HEAD_DIM:HEAD_DIM + 1, :]
    lam = _diff_lambda(lam_ref, lam_init)
    o = acc[:HEAD_DIM, :tq] / norm[:, :tq] - lam * (acc[:HEAD_DIM, tq:] / norm[:, tq:])
    ms = jnp.mean(o * o, axis=0, keepdims=True)
    o = o * lax.rsqrt(ms + SUBLN_EPS) * g_ref[...] * (1.0 - lam_init)
    o_ref[0, 0] = o.astype(o_ref.dtype)


def _diff_prompt(qt, k, vt1, slopes2, lam_p, subln_g, lam_init):
    b, _, _, t = qt.shape
    tq = DIFF_QROWS
    tk = DIFF_KROWS
    assert t % tq == 0 and tq % tk == 0 and tk % CHUNK == 0
    vrows = vt1.shape[2]
    return pl.pallas_call(
        functools.partial(_diff_prompt_kernel, tq=tq, tk=tk, lam_init=lam_init),
        grid=(b, N_HEADS, t // tq),
        in_specs=[pl.BlockSpec(memory_space=pltpu.SMEM),
                  pl.BlockSpec((1, 4, DIFF_DIM), lambda bi, hi, qi: (0, 0, 0)),
                  pl.BlockSpec((HEAD_DIM, 1), lambda bi, hi, qi: (0, 0)),
                  pl.BlockSpec((1, 1, HEAD_DIM, tq), lambda bi, hi, qi: (bi, hi, 0, qi)),
                  pl.BlockSpec((1, t, HEAD_DIM), lambda bi, hi, qi: (bi, 0, hi)),
                  pl.BlockSpec((1, 1, vrows, t), lambda bi, hi, qi: (bi, hi, 0, 0))],
        out_specs=pl.BlockSpec((1, 1, HEAD_DIM, tq), lambda bi, hi, qi: (bi, hi, 0, qi)),
        out_shape=jax.ShapeDtypeStruct((b, N_HEADS, HEAD_DIM, t), BF16),
        scratch_shapes=[pltpu.VMEM((vrows, 2 * tq), F32), pltpu.VMEM((1, HEAD_DIM), F32)],
        compiler_params=_params("arbitrary", "arbitrary", "arbitrary"),
        name="diff_prompt",
    )(slopes2, lam_p.reshape(1, 4, DIFF_DIM), subln_g.reshape(HEAD_DIM, 1), qt, k, vt1)


def _softmax_parts(scores, values):
    m = functools.reduce(jnp.maximum, [jnp.max(s, axis=-1, keepdims=True) for s in scores])
    ps = [jnp.exp(s - m) for s in scores]
    l = sum(jnp.sum(p, axis=-1, keepdims=True) for p in ps)
    o = sum(_dot(p.astype(BF16), v) for p, v in zip(ps, values))
    return o, l


def _diff_sample_kernel(slope_ref, lam_ref, g_ref, q_ref, kn_ref, vn_ref, kc_ref, vc_ref, o_ref, *,
                        past, lam_init):
    h = pl.program_id(1)
    slope = slope_ref[h]
    q1, q2 = _split_q(q_ref[0])
    n = q1.shape[0]
    kc = kc_ref[0, 0].astype(BF16)
    vc = vc_ref[0, 0].astype(BF16)
    kn = kn_ref[0]
    vn = vn_ref[0]

    def bias(k_start, k_len):
        q_pos = past + _iota2((n, k_len), 0)
        k_pos = k_start + _iota2((n, k_len), 1)
        allowed = (k_pos // CHUNK) <= (q_pos // CHUNK)
        return jnp.where(allowed, -slope * jnp.abs(q_pos - k_pos).astype(F32), NEG_BIG)

    bias_c = bias(0, kc.shape[0])
    bias_n = bias(past, n)
    o1, l1 = _softmax_parts([_dot_nt(q1, kc) + bias_c, _dot_nt(q1, kn) + bias_n], [vc, vn])
    o2, l2 = _softmax_parts([_dot_nt(q2, kc) + bias_c, _dot_nt(q2, kn) + bias_n], [vc, vn])
    lam = _diff_lambda(lam_ref, lam_init)
    o_ref[0] = _diff_finish(o1, l1, o2, l2, lam, g_ref[...], lam_init).astype(o_ref.dtype)


def _sample_specs(n, past, layer):
    new_rows = lambda bi, hi: (bi, 0, hi)
    cache_rows = lambda bi, hi: (layer, bi, 0, hi)
    new = pl.BlockSpec((1, n, HEAD_DIM), new_rows)
    cache = pl.BlockSpec((1, 1, past, HEAD_DIM), cache_rows)
    return new, cache


def _diff_sample(q, k_new, v_new, cache_k, cache_v, layer, slopes, lam_p, subln_g, lam_init):
    b, n, e = q.shape
    past = cache_k.shape[2]
    new, cache = _sample_specs(n, past, layer)
    return pl.pallas_call(
        functools.partial(_diff_sample_kernel, past=past, lam_init=lam_init),
        grid=(b, N_HEADS),
        in_specs=[pl.BlockSpec(memory_space=pltpu.SMEM),
                  pl.BlockSpec((1, 4, DIFF_DIM), lambda bi, hi: (0, 0, 0)),
                  pl.BlockSpec((1, HEAD_DIM), lambda bi, hi: (0, 0)),
                  new, new, new, cache, cache],
        out_specs=new,
        out_shape=jax.ShapeDtypeStruct((b, n, e), BF16),
        compiler_params=_params("arbitrary", "arbitrary"),
        name="diff_sample",
    )(slopes, lam_p.reshape(1, 4, DIFF_DIM), subln_g.reshape(1, HEAD_DIM), q, k_new, v_new, cache_k, cache_v)


def _band_prompt_kernel(bias_ref, q_ref, k_ref, v_ref, o_ref, *, tq, span):
    qi = pl.program_id(2)
    start = pl.multiple_of(qi * tq, tq)
    k = k_ref[0, pl.ds(start, span), :]
    v = v_ref[0, pl.ds(start, span), :]
    q_pos = qi * tq + _iota2((tq, span), 0)
    k_pos = qi * tq - BAND_ROWS + _iota2((tq, span), 1)
    qc = q_pos // CHUNK
    kc = (k_pos + BAND_ROWS) // CHUNK - BAND_CHUNKS
    valid = (k_pos >= 0) & (kc <= qc) & (kc >= qc - BAND_CHUNKS)
    s = jnp.where(valid, _dot_nt(q_ref[0], k) + bias_ref[0], NEG_BIG)
    o, l = _softmax_parts([s], [v])
    o_ref[0] = (o / l).astype(o_ref.dtype)


def _band_prompt(q, k_pad, v_pad, bias):
    b, t, e = q.shape
    tq = bias.shape[1]
    span = bias.shape[2]
    head_rows = lambda bi, hi, qi: (bi, 0, hi)
    q_rows = lambda bi, hi, qi: (bi, qi, hi)
    return pl.pallas_call(
        functools.partial(_band_prompt_kernel, tq=tq, span=span),
        grid=(b, N_HEADS, t // tq),
        in_specs=[pl.BlockSpec((1, tq, span), lambda bi, hi, qi: (hi, 0, 0)),
                  pl.BlockSpec((1, tq, HEAD_DIM), q_rows),
                  pl.BlockSpec((1, t + BAND_ROWS, HEAD_DIM), head_rows),
                  pl.BlockSpec((1, t + BAND_ROWS, HEAD_DIM), head_rows)],
        out_specs=pl.BlockSpec((1, tq, HEAD_DIM), q_rows),
        out_shape=jax.ShapeDtypeStruct((b, t, e), BF16),
        compiler_params=_params("arbitrary", "arbitrary", "arbitrary"),
        name="band_prompt",
    )(bias, q, k_pad, v_pad)


def _band_sample_kernel(bc_ref, bn_ref, q_ref, kn_ref, vn_ref, kc_ref, vc_ref, o_ref, *, past):
    q = q_ref[0]
    n = q.shape[0]
    kc = kc_ref[0, 0].astype(BF16)
    vc = vc_ref[0, 0].astype(BF16)
    keep = kc.shape[0]

    def valid(k_start, k_len):
        q_pos = past + _iota2((n, k_len), 0)
        k_pos = k_start + _iota2((n, k_len), 1)
        qc = q_pos // CHUNK
        kc_ = k_pos // CHUNK
        return (k_pos >= 0) & (kc_ <= qc) & (kc_ >= qc - BAND_CHUNKS)

    s_c = jnp.where(valid(past - keep, keep), _dot_nt(q, kc) + bc_ref[0], NEG_BIG)
    s_n = jnp.where(valid(past, n), _dot_nt(q, kn_ref[0]) + bn_ref[0], NEG_BIG)
    o, l = _softmax_parts([s_c, s_n], [vc, vn_ref[0]])
    o_ref[0] = (o / l).astype(o_ref.dtype)


def _band_sample(q, k_new, v_new, cache_k, cache_v, layer, bias_c, bias_n, past):
    b, n, e = q.shape
    keep = cache_k.shape[2]
    new, cache = _sample_specs(n, keep, layer)
    return pl.pallas_call(
        functools.partial(_band_sample_kernel, past=past),
        grid=(b, N_HEADS),
        in_specs=[pl.BlockSpec((1, n, keep), lambda bi, hi: (hi, 0, 0)),
                  pl.BlockSpec((1, n, n), lambda bi, hi: (hi, 0, 0)),
                  new, new, new, cache, cache],
        out_specs=new,
        out_shape=jax.ShapeDtypeStruct((b, n, e), BF16),
        compiler_params=_params("arbitrary", "arbitrary"),
        name="band_sample",
    )(bias_c, bias_n, q, k_new, v_new, cache_k, cache_v)


def _band_bias(rel_bias, q_pos, k_pos):
    idx = jnp.clip(q_pos[:, None] - k_pos[None, :], -REL_CLIP, REL_CLIP) + REL_CLIP
    return rel_bias.astype(F32)[:, idx]


def _strict_upper(nk):
    return jnp.where(_iota2((nk, nk), 0) > _iota2((nk, nk), 1), 1.0, 0.0).astype(BF16)


def _stick_block(q, k, v, upper, carry, valid):
    z = _dot_nt(q, k)
    soft = jnp.log1p(jnp.exp(-jnp.abs(z)))
    log_beta = jnp.minimum(z, 0.0) - soft
    log_keep = -jnp.maximum(z, 0.0) - soft
    if valid is not None:
        log_keep = jnp.where(valid, log_keep, 0.0)
    hi = log_keep.astype(BF16)
    lo = (log_keep - hi.astype(F32)).astype(BF16)
    after = _dot(hi, upper) + _dot(lo, upper) + carry
    w = jnp.exp(log_beta + after)
    if valid is not None:
        w = jnp.where(valid, w, 0.0)
    out = _dot(w.astype(BF16), v)
    return out, carry + jnp.sum(log_keep, axis=-1, keepdims=True)


def _stick_prompt_kernel(q_ref, k_ref, v_ref, o_ref, *, blk):
    qi = pl.program_id(2)
    q = q_ref[0]
    upper = _strict_upper(blk)

    start = pl.multiple_of(qi * blk, blk)
    causal = _iota2((blk, blk), 1) < _iota2((blk, blk), 0)
    acc, carry = _stick_block(q, k_ref[0, pl.ds(start, blk), :], v_ref[0, pl.ds(start, blk), :], upper,
                              jnp.zeros((blk, 1), F32), causal)

    def cond(state):
        j, alive, _, _ = state
        return (j >= 0) & (alive > STICK_DEAD)

    def body(state):
        j, _, acc, carry = state
        start = pl.multiple_of(j * blk, blk)
        out, carry = _stick_block(q, k_ref[0, pl.ds(start, blk), :], v_ref[0, pl.ds(start, blk), :], upper,
                                  carry, None)
        return j - 1, jnp.max(carry), acc + out, carry

    _, _, acc, _ = lax.while_loop(cond, body, (qi - 1, jnp.max(carry), acc, carry))
    o_ref[0] = acc.astype(o_ref.dtype)


def _stick_prompt(q, k, v):
    b, t, e = q.shape
    blk = min(STICK_BLOCK, t)
    head_rows = lambda bi, hi, qi: (bi, 0, hi)
    q_rows = lambda bi, hi, qi: (bi, qi, hi)
    return pl.pallas_call(
        functools.partial(_stick_prompt_kernel, blk=blk),
        grid=(b, N_HEADS, t // blk),
        in_specs=[pl.BlockSpec((1, blk, HEAD_DIM), q_rows),
                  pl.BlockSpec((1, t, HEAD_DIM), head_rows),
                  pl.BlockSpec((1, t, HEAD_DIM), head_rows)],
        out_specs=pl.BlockSpec((1, blk, HEAD_DIM), q_rows),
        out_shape=jax.ShapeDtypeStruct((b, t, e), BF16),
        compiler_params=_params("arbitrary", "arbitrary", "arbitrary"),
        name="stick_prompt",
    )(q, k, v)


def _stick_sample_kernel(q_ref, kn_ref, vn_ref, kc_ref, vc_ref, o_ref, *, blk):
    q = q_ref[0]
    n = q.shape[0]
    past = kc_ref.shape[2]
    upper = _strict_upper(blk)
    causal = _iota2((n, n), 1) < _iota2((n, n), 0)
    acc, carry = _stick_block(q, kn_ref[0], vn_ref[0], upper[:n, :n], jnp.zeros((n, 1), F32), causal)
    for j in reversed(range(past // blk)):
        kc = kc_ref[0, 0, j * blk:(j + 1) * blk, :].astype(BF16)
        vc = vc_ref[0, 0, j * blk:(j + 1) * blk, :].astype(BF16)
        out, carry = _stick_block(q, kc, vc, upper, carry, None)
        acc = acc + out
    o_ref[0] = acc.astype(o_ref.dtype)


def _stick_sample(q, k_new, v_new, cache_k, cache_v, layer):
    b, n, e = q.shape
    past = cache_k.shape[2]
    blk = min(256, past)
    assert past % blk == 0
    new, cache = _sample_specs(n, past, layer)
    return pl.pallas_call(
        functools.partial(_stick_sample_kernel, blk=blk),
        grid=(b, N_HEADS),
        in_specs=[new, new, new, cache, cache],
        out_specs=new,
        out_shape=jax.ShapeDtypeStruct((b, n, e), BF16),
        compiler_params=_params("arbitrary", "arbitrary"),
        name="stick_sample",
    )(q, k_new, v_new, cache_k, cache_v)


def kernel(x_prompt, x_sample, cache_a_k, cache_a_v, cache_b_k, cache_b_v, cache_c_k, cache_c_v,
           norm_g, w_in, w_out, a_lambda, a_subln_g, b_rel_bias, final_g):
    bp, t, d = x_prompt.shape
    bs, n, _ = x_sample.shape
    depth = w_in.shape[0]
    e = w_out.shape[1]
    past = cache_a_k.shape[2]
    keep_s = cache_b_k.shape[2]
    keep_p = min(BAND_ROWS, t)
    heads5 = lambda a, b_: a.reshape(b_, -1, N_HEADS, HEAD_DIM)
    flat_cache = lambda c: c.reshape(c.shape[0], c.shape[1], c.shape[2], e)

    slopes = 2.0 ** (-8.0 * jnp.arange(1, N_HEADS + 1, dtype=F32) / N_HEADS)
    w_in_bf = w_in.astype(BF16)
    w_out_bf = w_out.astype(BF16)
    caches = [(flat_cache(cache_a_k), flat_cache(cache_a_v)),
              (flat_cache(cache_b_k), flat_cache(cache_b_v)),
              (flat_cache(cache_c_k), flat_cache(cache_c_v))]

    xp = x_prompt.reshape(bp * t, d)
    xs = x_sample.reshape(bs * n, d)
    new_p = [([], []) for _ in range(N_MIXERS)]
    new_s = [([], []) for _ in range(N_MIXERS)]
    for i in range(depth):
        kind = i % N_MIXERS
        j = i // N_MIXERS
        last = i == depth - 1
        qscale = DIFF_DIM ** -0.5 if kind == 0 else HEAD_DIM ** -0.5
        qscale_p = qscale * LOG2E if kind == 0 else qscale
        qp, kp, vp, kpb, vpb, sgp = _in_proj(xp, norm_g[i], w_in_bf[i], qscale_p)
        qs, ks, vs, ksb, vsb, sgs = _in_proj(xs, norm_g[i], w_in_bf[i], qscale)
        seq_p = lambda a: a.reshape(bp, t, e)
        seq_s = lambda a: a.reshape(bs, n, e)
        cache_k, cache_v = caches[kind]
        if kind == 0:
            lam_init = 0.8 - 0.6 * math.exp(-0.3 * i)
            head_major = lambda a: heads5(a, bp).transpose(0, 2, 3, 1)
            vt1 = jnp.concatenate([head_major(vpb), jnp.ones((bp, N_HEADS, ONES_ROWS, t), BF16)], axis=2)
            op_t = _diff_prompt(head_major(qp), seq_p(kpb), vt1, slopes * LOG2E, a_lambda[j], a_subln_g[j],
                                lam_init)
            op = op_t.transpose(0, 3, 1, 2)
            os_ = _diff_sample(seq_s(qs), seq_s(ksb), seq_s(vsb), cache_k, cache_v, j, slopes,
                               a_lambda[j], a_subln_g[j], lam_init)
            new_p[0][0].append(heads5(kp, bp))
            new_p[0][1].append(heads5(vp, bp))
            new_s[0][0].append(heads5(ks, bs))
            new_s[0][1].append(heads5(vs, bs))
        elif kind == 1:
            tq = min(BAND_QROWS, t)
            span = BAND_ROWS + tq
            bias_p = _band_bias(b_rel_bias[j], jnp.arange(tq, dtype=jnp.int32),
                                jnp.arange(span, dtype=jnp.int32) - BAND_ROWS)
            pad = ((0, 0), (BAND_ROWS, 0), (0, 0))
            op = _band_prompt(seq_p(qp), jnp.pad(seq_p(kpb), pad), jnp.pad(seq_p(vpb), pad), bias_p)
            pos_s = past + jnp.arange(n, dtype=jnp.int32)
            bias_c = _band_bias(b_rel_bias[j], pos_s, past - keep_s + jnp.arange(keep_s, dtype=jnp.int32))
            bias_n = _band_bias(b_rel_bias[j], pos_s, pos_s)
            os_ = _band_sample(seq_s(qs), seq_s(ksb), seq_s(vsb), cache_k, cache_v, j, bias_c, bias_n, past)
            new_p[1][0].append(heads5(kp, bp)[:, t - keep_p:])
            new_p[1][1].append(heads5(vp, bp)[:, t - keep_p:])
            new_s[1][0].append(jnp.concatenate([cache_b_k[j], heads5(ks, bs)], axis=1)[:, n:])
            new_s[1][1].append(jnp.concatenate([cache_b_v[j], heads5(vs, bs)], axis=1)[:, n:])
        else:
            op = _stick_prompt(seq_p(qp), seq_p(kpb), seq_p(vpb))
            os_ = _stick_sample(seq_s(qs), seq_s(ksb), seq_s(vsb), cache_k, cache_v, j)
            new_p[2][0].append(heads5(kp, bp))
            new_p[2][1].append(heads5(vp, bp))
            new_s[2][0].append(heads5(ks, bs))
            new_s[2][1].append(heads5(vs, bs))
        xp = _out_proj(op.reshape(bp * t, e), sgp, w_out_bf[i], xp, final_g, last)
        xs = _out_proj(os_.reshape(bs * n, e), sgs, w_out_bf[i], xs, final_g, last)

    outs = [xp.reshape(bp, t, d), xs.reshape(bs, n, d)]
    for group in (new_p, new_s):
        for ks_list, vs_list in group:
            outs.append(jnp.stack(ks_list))
            outs.append(jnp.stack(vs_list))
    return tuple(outs)
```

```python
import functools
import math

import jax
import jax.numpy as jnp
from jax import lax
from jax.experimental import pallas as pl
from jax.experimental.pallas import tpu as pltpu

F32 = jnp.float32
BF16 = jnp.bfloat16

CHUNK = 64
N_MIXERS = 3
N_HEADS = 16
HEAD_DIM = 128
DIFF_DIM = HEAD_DIM // 2
BAND_CHUNKS = 8
BAND_ROWS = BAND_CHUNKS * CHUNK
REL_CLIP = 128
EPS = 1e-6
SUBLN_EPS = 1e-5

VMEM_LIMIT_BYTES = 56 * 1024 * 1024
NEG_BIG = -1e30

IN_PROJ_ROWS = 256
OUT_PROJ_ROWS = 512
DIFF_QROWS = 1024
DIFF_KROWS = 256
DIFF_LANE_CHUNK = 2 * DIFF_QROWS
ONES_ROWS = 16
DIFF_DEAD = -160.0
STICK_DEAD = -160.0
LOG2E = math.log2(math.e)
BAND_QROWS = 256
STICK_QROWS = 1024
STICK_KROWS = 256


def _params(*sem):
    return pltpu.CompilerParams(dimension_semantics=sem, vmem_limit_bytes=VMEM_LIMIT_BYTES)


def _dot(a, b):
    return jnp.dot(a, b, preferred_element_type=F32)


def _dot_nt(a, b):
    return lax.dot_general(a, b, (((1,), (1,)), ((), ())), preferred_element_type=F32)


def _iota2(shape, dim):
    return lax.broadcasted_iota(jnp.int32, shape, dim)


def _in_proj_kernel(x_ref, g_ref, w_ref, q_ref, k_ref, v_ref, kb_ref, vb_ref, sg_ref, *, qscale, width):
    x = x_ref[...]
    ms = jnp.mean(x * x, axis=-1, keepdims=True)
    xn = (x * lax.rsqrt(ms + EPS) * g_ref[...]).astype(BF16)
    q = _dot(xn, w_ref[:, 0:width])
    q_ref[...] = (q * qscale).astype(BF16)
    k = _dot(xn, w_ref[:, width:2 * width])
    k_ref[...] = k
    kb_ref[...] = k.astype(BF16)
    v = _dot(xn, w_ref[:, 2 * width:3 * width])
    v_ref[...] = v
    vb_ref[...] = v.astype(BF16)
    gate = _dot(xn, w_ref[:, 3 * width:4 * width])
    sg_ref[...] = (gate * jax.nn.sigmoid(gate)).astype(BF16)


def _in_proj(x2d, g, w_bf, qscale):
    m, d = x2d.shape
    width = w_bf.shape[1] // 4
    tm = min(IN_PROJ_ROWS, m)
    row = lambda i: (i, 0)
    fixed = lambda i: (0, 0)
    out_block = pl.BlockSpec((tm, width), row)
    return pl.pallas_call(
        functools.partial(_in_proj_kernel, qscale=qscale, width=width),
        grid=(m // tm,),
        in_specs=[pl.BlockSpec((tm, d), row),
                  pl.BlockSpec((1, d), fixed),
                  pl.BlockSpec((d, 4 * width), fixed, pipeline_mode=pl.Buffered(1))],
        out_specs=[out_block] * 6,
        out_shape=[jax.ShapeDtypeStruct((m, width), BF16),
                   jax.ShapeDtypeStruct((m, width), F32),
                   jax.ShapeDtypeStruct((m, width), F32),
                   jax.ShapeDtypeStruct((m, width), BF16),
                   jax.ShapeDtypeStruct((m, width), BF16),
                   jax.ShapeDtypeStruct((m, width), BF16)],
        compiler_params=_params("arbitrary"),
        name="in_proj",
    )(x2d, g.reshape(1, d), w_bf)


def _out_proj_kernel(o_ref, sg_ref, w_ref, x_ref, fg_ref, y_ref, *, final):
    a = o_ref[...] * sg_ref[...]
    y = x_ref[...] + _dot(a, w_ref[...])
    if final:
        ms = jnp.mean(y * y, axis=-1, keepdims=True)
        y = y * lax.rsqrt(ms + EPS) * fg_ref[...]
    y_ref[...] = y


def _out_proj(o2d, sg2d, w_bf, x2d, final_g, final):
    m, e = o2d.shape
    d = x2d.shape[1]
    tm = min(OUT_PROJ_ROWS, m)
    row = lambda i: (i, 0)
    fixed = lambda i: (0, 0)
    return pl.pallas_call(
        functools.partial(_out_proj_kernel, final=final),
        grid=(m // tm,),
        in_specs=[pl.BlockSpec((tm, e), row),
                  pl.BlockSpec((tm, e), row),
                  pl.BlockSpec((e, d), fixed),
                  pl.BlockSpec((tm, d), row),
                  pl.BlockSpec((1, d), fixed)],
        out_specs=pl.BlockSpec((tm, d), row),
        out_shape=jax.ShapeDtypeStruct((m, d), F32),
        compiler_params=_params("arbitrary"),
        name="out_proj_final" if final else "out_proj",
    )(o2d, sg2d, w_bf, x2d, final_g.reshape(1, d))


def _split_q(q):
    lane = _iota2(q.shape, 1)
    zero = jnp.zeros_like(q)
    return jnp.where(lane < DIFF_DIM, q, zero), jnp.where(lane >= DIFF_DIM, q, zero)


def _diff_lambda(lam_ref, lam_init):
    lp = lam_ref[0]
    e1 = jnp.exp(jnp.sum(lp[0:1, :] * lp[1:2, :], axis=1, keepdims=True))
    e2 = jnp.exp(jnp.sum(lp[2:3, :] * lp[3:4, :], axis=1, keepdims=True))
    return e1 - e2 + lam_init


def _diff_finish(o1, l1, o2, l2, lam, g, lam_init):
    o = o1 / l1 - lam * (o2 / l2)
    ms = jnp.mean(o * o, axis=-1, keepdims=True)
    return o * lax.rsqrt(ms + SUBLN_EPS) * g * (1.0 - lam_init)


def _diff_prompt_kernel(slope_ref, lam_ref, g_ref, qt_ref, k_ref, vt_ref, o_ref, acc_ref, kn_ref, *,
                        tq, tk, lam_init):
    h = pl.program_id(1)
    qb = pl.program_id(2)
    slope2 = slope_ref[h]
    t_keys = k_ref.shape[1]
    lanes = 2 * tq
    nsub = tq // tk
    step = slope2 * tk

    @pl.when(qb == 0)
    def _():
        same_map = (_iota2((HEAD_DIM, HEAD_DIM), 0) < DIFF_DIM) == (_iota2((HEAD_DIM, HEAD_DIM), 1) < DIFF_DIM)
        sel = jnp.where(same_map, 1.0, 0.0).astype(BF16)

        def body(i, mx):
            kb = k_ref[0, pl.ds(pl.multiple_of(i * tk, tk), tk), :].astype(F32)
            norm2 = _dot((kb * kb).astype(BF16), sel)
            return jnp.maximum(mx, jnp.max(norm2, axis=0, keepdims=True))
        mx = lax.fori_loop(0, t_keys // tk, body, jnp.zeros((1, HEAD_DIM), F32))
        kn_ref[...] = jnp.sqrt(mx * 1.01)

    qt = qt_ref[0, 0]
    row = _iota2(qt.shape, 0)
    zero = jnp.zeros_like(qt)
    q2t = jnp.concatenate([jnp.where(row < DIFF_DIM, qt, zero), jnp.where(row >= DIFF_DIM, qt, zero)], axis=1)
    qf = q2t.astype(F32)
    qn = jnp.sqrt(jnp.sum(qf * qf, axis=0, keepdims=True))
    kn = jnp.where(_iota2((1, lanes), 1) < tq, kn_ref[0:1, 0:1], kn_ref[0:1, DIFF_DIM:DIFF_DIM + 1])
    bound = qn * kn * 1.001 + 1.0 + slope2 * (tk - 1)

    acc_ref[...] = jnp.zeros_like(acc_ref)

    def visit(k_start, bias, m, lo, hi):
        kblk = k_ref[0, pl.ds(k_start, tk), :]
        vtblk = jnp.concatenate([vt_ref[0, 0, :, pl.ds(k_start, tk)], jnp.ones((ONES_ROWS, tk), BF16)], axis=0)
        st = _dot(kblk, q2t[:, lo:hi]) + bias
        m_new = jnp.maximum(m, jnp.max(st, axis=0, keepdims=True))
        alpha = jnp.exp2(m - m_new)
        p = jnp.exp2(st - m_new).astype(BF16)
        acc_ref[:, lo:hi] = alpha * acc_ref[:, lo:hi] + _dot(vtblk, p)
        return m_new

    jl = _iota2((tk, tq), 0)
    iq = _iota2((tk, tq), 1)
    rel = (iq - jl).astype(F32)
    diag_bias = jnp.where((jl // CHUNK) <= (iq // CHUNK), slope2 * ((rel - jnp.abs(rel)) + jl.astype(F32)),
                          2.0 * NEG_BIG)
    m_maps = [jnp.full((1, tq), NEG_BIG, F32), jnp.full((1, tq), NEG_BIG, F32)]
    q0 = pl.multiple_of(qb * tq, tq)
    for sub in reversed(range(nsub)):
        off = sub * tk
        for mp in range(2):
            seen = visit(q0 + off, diag_bias[:, :tq - off], m_maps[mp][:, off:] + step, mp * tq + off, (mp + 1) * tq)
            m_maps[mp] = jnp.concatenate([m_maps[mp][:, :off] + step, seen], axis=1) if off else seen
    m = jnp.concatenate(m_maps, axis=1)

    width = DIFF_LANE_CHUNK
    in_block = slope2 * _iota2((tk, width), 0).astype(F32)

    def cond(state):
        j, worst, _ = state
        return (j >= 0) & (worst > DIFF_DEAD)

    def body(state):
        j, _, m = state
        k_start = pl.multiple_of(j * tk, tk)
        m = jnp.concatenate([visit(k_start, in_block, m[:, lo:lo + width] + step, lo, lo + width)
                             for lo in range(0, lanes, width)], axis=1)
        return j - 1, jnp.max(bound - (m + step)), m

    lax.while_loop(cond, body, (qb * nsub - 1, jnp.max(bound - (m + step)), m))

    acc = acc_ref[...]
    norm = acc[HEAD_DIM:HEAD_DIM + 1, :]
    lam = _diff_lambda(lam_ref, lam_init)
    o = acc[:HEAD_DIM, :tq] / norm[:, :tq] - lam * (acc[:HEAD_DIM, tq:] / norm[:, tq:])
    ms = jnp.mean(o * o, axis=0, keepdims=True)
    o = o * lax.rsqrt(ms + SUBLN_EPS) * g_ref[...] * (1.0 - lam_init)
    o_ref[0, 0] = o.astype(o_ref.dtype)


def _diff_prompt(qt, k, vt, slopes2, lam_p, subln_g, lam_init):
    b, _, _, t = qt.shape
    tq = DIFF_QROWS
    tk = DIFF_KROWS
    assert t % tq == 0 and tq % tk == 0 and tk % CHUNK == 0
    vrows = HEAD_DIM + ONES_ROWS
    return pl.pallas_call(
        functools.partial(_diff_prompt_kernel, tq=tq, tk=tk, lam_init=lam_init),
        grid=(b, N_HEADS, t // tq),
        in_specs=[pl.BlockSpec(memory_space=pltpu.SMEM),
                  pl.BlockSpec((1, 4, DIFF_DIM), lambda bi, hi, qi: (0, 0, 0)),
                  pl.BlockSpec((HEAD_DIM, 1), lambda bi, hi, qi: (0, 0)),
                  pl.BlockSpec((1, 1, HEAD_DIM, tq), lambda bi, hi, qi: (bi, hi, 0, qi)),
                  pl.BlockSpec((1, t, HEAD_DIM), lambda bi, hi, qi: (bi, 0, hi)),
                  pl.BlockSpec((1, 1, HEAD_DIM, t), lambda bi, hi, qi: (bi, hi, 0, 0))],
        out_specs=pl.BlockSpec((1, 1, HEAD_DIM, tq), lambda bi, hi, qi: (bi, hi, 0, qi)),
        out_shape=jax.ShapeDtypeStruct((b, N_HEADS, HEAD_DIM, t), BF16),
        scratch_shapes=[pltpu.VMEM((vrows, 2 * tq), F32), pltpu.VMEM((1, HEAD_DIM), F32)],
        compiler_params=_params("arbitrary", "arbitrary", "arbitrary"),
        name="diff_prompt",
    )(slopes2, lam_p.reshape(1, 4, DIFF_DIM), subln_g.reshape(HEAD_DIM, 1), qt, k, vt)


def _softmax_parts(scores, values):
    m = functools.reduce(jnp.maximum, [jnp.max(s, axis=-1, keepdims=True) for s in scores])
    ps = [jnp.exp(s - m) for s in scores]
    l = sum(jnp.sum(p, axis=-1, keepdims=True) for p in ps)
    o = sum(_dot(p.astype(BF16), v) for p, v in zip(ps, values))
    return o, l


def _diff_sample_kernel(slope_ref, lam_ref, g_ref, q_ref, kn_ref, vn_ref, kc_ref, vc_ref, o_ref, *,
                        past, lam_init):
    h = pl.program_id(1)
    slope = slope_ref[h]
    q1, q2 = _split_q(q_ref[0])
    n = q1.shape[0]
    kc = kc_ref[0, 0].astype(BF16)
    vc = vc_ref[0, 0].astype(BF16)
    kn = kn_ref[0]
    vn = vn_ref[0]

    def bias(k_start, k_len):
        q_pos = past + _iota2((n, k_len), 0)
        k_pos = k_start + _iota2((n, k_len), 1)
        allowed = (k_pos // CHUNK) <= (q_pos // CHUNK)
        return jnp.where(allowed, -slope * jnp.abs(q_pos - k_pos).astype(F32), NEG_BIG)

    bias_c = bias(0, kc.shape[0])
    bias_n = bias(past, n)
    o1, l1 = _softmax_parts([_dot_nt(q1, kc) + bias_c, _dot_nt(q1, kn) + bias_n], [vc, vn])
    o2, l2 = _softmax_parts([_dot_nt(q2, kc) + bias_c, _dot_nt(q2, kn) + bias_n], [vc, vn])
    lam = _diff_lambda(lam_ref, lam_init)
    o_ref[0] = _diff_finish(o1, l1, o2, l2, lam, g_ref[...], lam_init).astype(o_ref.dtype)


def _sample_specs(n, past, layer):
    new_rows = lambda bi, hi: (bi, 0, hi)
    cache_rows = lambda bi, hi: (layer, bi, 0, hi)
    new = pl.BlockSpec((1, n, HEAD_DIM), new_rows)
    cache = pl.BlockSpec((1, 1, past, HEAD_DIM), cache_rows)
    return new, cache


def _diff_sample(q, k_new, v_new, cache_k, cache_v, layer, slopes, lam_p, subln_g, lam_init):
    b, n, e = q.shape
    past = cache_k.shape[2]
    new, cache = _sample_specs(n, past, layer)
    return pl.pallas_call(
        functools.partial(_diff_sample_kernel, past=past, lam_init=lam_init),
        grid=(b, N_HEADS),
        in_specs=[pl.BlockSpec(memory_space=pltpu.SMEM),
                  pl.BlockSpec((1, 4, DIFF_DIM), lambda bi, hi: (0, 0, 0)),
                  pl.BlockSpec((1, HEAD_DIM), lambda bi, hi: (0, 0)),
                  new, new, new, cache, cache],
        out_specs=new,
        out_shape=jax.ShapeDtypeStruct((b, n, e), BF16),
        compiler_params=_params("arbitrary", "arbitrary"),
        name="diff_sample",
    )(slopes, lam_p.reshape(1, 4, DIFF_DIM), subln_g.reshape(1, HEAD_DIM), q, k_new, v_new, cache_k, cache_v)


def _band_prompt_kernel(bias_ref, q_ref, k_ref, v_ref, o_ref, *, tq, span):
    qi = pl.program_id(2)
    start = pl.multiple_of(qi * tq, tq)
    k = k_ref[0, pl.ds(start, span), :]
    v = v_ref[0, pl.ds(start, span), :]
    in_sequence = qi * tq - BAND_ROWS + _iota2((1, span), 1) >= 0
    s = jnp.where(in_sequence, _dot_nt(q_ref[0], k) + bias_ref[0], NEG_BIG)
    o, l = _softmax_parts([s], [v])
    o_ref[0] = (o / l).astype(o_ref.dtype)


def _band_prompt(q, k_pad, v_pad, bias):
    b, t, e = q.shape
    tq = bias.shape[1]
    span = bias.shape[2]
    head_rows = lambda bi, hi, qi: (bi, 0, hi)
    q_rows = lambda bi, hi, qi: (bi, qi, hi)
    return pl.pallas_call(
        functools.partial(_band_prompt_kernel, tq=tq, span=span),
        grid=(b, N_HEADS, t // tq),
        in_specs=[pl.BlockSpec((1, tq, span), lambda bi, hi, qi: (hi, 0, 0)),
                  pl.BlockSpec((1, tq, HEAD_DIM), q_rows),
                  pl.BlockSpec((1, t + BAND_ROWS, HEAD_DIM), head_rows),
                  pl.BlockSpec((1, t + BAND_ROWS, HEAD_DIM), head_rows)],
        out_specs=pl.BlockSpec((1, tq, HEAD_DIM), q_rows),
        out_shape=jax.ShapeDtypeStruct((b, t, e), BF16),
        compiler_params=_params("arbitrary", "arbitrary", "arbitrary"),
        name="band_prompt",
    )(bias, q, k_pad, v_pad)


def _band_sample_kernel(bc_ref, bn_ref, q_ref, kn_ref, vn_ref, kc_ref, vc_ref, o_ref, *, past):
    q = q_ref[0]
    n = q.shape[0]
    kc = kc_ref[0, 0].astype(BF16)
    vc = vc_ref[0, 0].astype(BF16)
    keep = kc.shape[0]

    def valid(k_start, k_len):
        q_pos = past + _iota2((n, k_len), 0)
        k_pos = k_start + _iota2((n, k_len), 1)
        qc = q_pos // CHUNK
        kc_ = k_pos // CHUNK
        return (k_pos >= 0) & (kc_ <= qc) & (kc_ >= qc - BAND_CHUNKS)

    s_c = jnp.where(valid(past - keep, keep), _dot_nt(q, kc) + bc_ref[0], NEG_BIG)
    s_n = jnp.where(valid(past, n), _dot_nt(q, kn_ref[0]) + bn_ref[0], NEG_BIG)
    o, l = _softmax_parts([s_c, s_n], [vc, vn_ref[0]])
    o_ref[0] = (o / l).astype(o_ref.dtype)


def _band_sample(q, k_new, v_new, cache_k, cache_v, layer, bias_c, bias_n, past):
    b, n, e = q.shape
    keep = cache_k.shape[2]
    new, cache = _sample_specs(n, keep, layer)
    return pl.pallas_call(
        functools.partial(_band_sample_kernel, past=past),
        grid=(b, N_HEADS),
        in_specs=[pl.BlockSpec((1, n, keep), lambda bi, hi: (hi, 0, 0)),
                  pl.BlockSpec((1, n, n), lambda bi, hi: (hi, 0, 0)),
                  new, new, new, cache, cache],
        out_specs=new,
        out_shape=jax.ShapeDtypeStruct((b, n, e), BF16),
        compiler_params=_params("arbitrary", "arbitrary"),
        name="band_sample",
    )(bias_c, bias_n, q, k_new, v_new, cache_k, cache_v)


def _band_bias(rel_bias, q_pos, k_pos):
    idx = jnp.clip(q_pos[:, None] - k_pos[None, :], -REL_CLIP, REL_CLIP) + REL_CLIP
    return rel_bias.astype(F32)[:, idx]


def _strict_upper(nk):
    return jnp.where(_iota2((nk, nk), 0) > _iota2((nk, nk), 1), 1.0, 0.0).astype(BF16)


def _stick_block(q, k, v, upper, carry, valid):
    z = _dot_nt(q, k)
    soft = jnp.log2(1.0 + jnp.exp2(-jnp.abs(z)))
    log_beta = jnp.minimum(z, 0.0) - soft
    log_keep = -jnp.maximum(z, 0.0) - soft
    if valid is not None:
        log_keep = jnp.where(valid, log_keep, 0.0)
    hi = log_keep.astype(BF16)
    lo = (log_keep - hi.astype(F32)).astype(BF16)
    after = _dot(hi, upper) + _dot(lo, upper) + carry
    w = jnp.exp2(log_beta + after)
    if valid is not None:
        w = jnp.where(valid, w, 0.0)
    out = _dot(w.astype(BF16), v)
    return out, carry + jnp.sum(log_keep, axis=-1, keepdims=True)


def _stick_prompt_kernel(q_ref, k_ref, v_ref, o_ref, acc_ref, carry_ref, *, tq, tk):
    qb = pl.program_id(2)
    upper = _strict_upper(tk)
    q0 = pl.multiple_of(qb * tq, tq)
    acc_ref[...] = jnp.zeros_like(acc_ref)
    carry_ref[...] = jnp.zeros_like(carry_ref)

    def fold(k_start, r0, valid):
        out, carry = _stick_block(q_ref[0, r0:, :], k_ref[0, pl.ds(k_start, tk), :], v_ref[0, pl.ds(k_start, tk), :],
                                  upper, carry_ref[r0:, :], valid)
        acc_ref[r0:, :] += out
        carry_ref[r0:, :] = carry
        return carry

    for r0 in reversed(range(0, tq, tk)):
        rows = tq - r0
        fold(q0 + r0, r0, _iota2((rows, tk), 1) < _iota2((rows, tk), 0))

    def cond(state):
        j, alive = state
        return (j >= 0) & (alive > STICK_DEAD)

    def body(state):
        j, _ = state
        carry = fold(pl.multiple_of(j * tk, tk), 0, None)
        return j - 1, jnp.max(carry)

    lax.while_loop(cond, body, (qb * (tq // tk) - 1, jnp.max(carry_ref[...])))
    o_ref[0] = acc_ref[...].astype(o_ref.dtype)


def _stick_prompt(q, k, v):
    b, t, e = q.shape
    tq = STICK_QROWS
    tk = STICK_KROWS
    assert t % tq == 0 and tq % tk == 0
    head_rows = lambda bi, hi, qi: (bi, 0, hi)
    q_rows = lambda bi, hi, qi: (bi, qi, hi)
    return pl.pallas_call(
        functools.partial(_stick_prompt_kernel, tq=tq, tk=tk),
        grid=(b, N_HEADS, t // tq),
        in_specs=[pl.BlockSpec((1, tq, HEAD_DIM), q_rows),
                  pl.BlockSpec((1, t, HEAD_DIM), head_rows),
                  pl.BlockSpec((1, t, HEAD_DIM), head_rows)],
        out_specs=pl.BlockSpec((1, tq, HEAD_DIM), q_rows),
        out_shape=jax.ShapeDtypeStruct((b, t, e), BF16),
        scratch_shapes=[pltpu.VMEM((tq, HEAD_DIM), F32), pltpu.VMEM((tq, 1), F32)],
        compiler_params=_params("arbitrary", "arbitrary", "arbitrary"),
        name="stick_prompt",
    )(q, k, v)


def _stick_sample_kernel(q_ref, kn_ref, vn_ref, kc_ref, vc_ref, o_ref, *, blk):
    q = q_ref[0]
    n = q.shape[0]
    past = kc_ref.shape[2]
    upper = _strict_upper(blk)
    causal = _iota2((n, n), 1) < _iota2((n, n), 0)
    acc, carry = _stick_block(q, kn_ref[0], vn_ref[0], upper[:n, :n], jnp.zeros((n, 1), F32), causal)
    for j in reversed(range(past // blk)):
        kc = kc_ref[0, 0, j * blk:(j + 1) * blk, :].astype(BF16)
        vc = vc_ref[0, 0, j * blk:(j + 1) * blk, :].astype(BF16)
        out, carry = _stick_block(q, kc, vc, upper, carry, None)
        acc = acc + out
    o_ref[0] = acc.astype(o_ref.dtype)


def _stick_sample(q, k_new, v_new, cache_k, cache_v, layer):
    b, n, e = q.shape
    past = cache_k.shape[2]
    blk = min(256, past)
    assert past % blk == 0
    new, cache = _sample_specs(n, past, layer)
    return pl.pallas_call(
        functools.partial(_stick_sample_kernel, blk=blk),
        grid=(b, N_HEADS),
        in_specs=[new, new, new, cache, cache],
        out_specs=new,
        out_shape=jax.ShapeDtypeStruct((b, n, e), BF16),
        compiler_params=_params("arbitrary", "arbitrary"),
        name="stick_sample",
    )(q, k_new, v_new, cache_k, cache_v)


def kernel(x_prompt, x_sample, cache_a_k, cache_a_v, cache_b_k, cache_b_v, cache_c_k, cache_c_v,
           norm_g, w_in, w_out, a_lambda, a_subln_g, b_rel_bias, final_g):
    bp, t, d = x_prompt.shape
    bs, n, _ = x_sample.shape
    depth = w_in.shape[0]
    e = w_out.shape[1]
    past = cache_a_k.shape[2]
    keep_s = cache_b_k.shape[2]
    keep_p = min(BAND_ROWS, t)
    heads5 = lambda a, b_: a.reshape(b_, -1, N_HEADS, HEAD_DIM)
    flat_cache = lambda c: c.reshape(c.shape[0], c.shape[1], c.shape[2], e)

    slopes = 2.0 ** (-8.0 * jnp.arange(1, N_HEADS + 1, dtype=F32) / N_HEADS)
    w_in_bf = w_in.astype(BF16)
    w_out_bf = w_out.astype(BF16)
    caches = [(flat_cache(cache_a_k), flat_cache(cache_a_v)),
              (flat_cache(cache_b_k), flat_cache(cache_b_v)),
              (flat_cache(cache_c_k), flat_cache(cache_c_v))]

    xp = x_prompt.reshape(bp * t, d)
    xs = x_sample.reshape(bs * n, d)
    new_p = [([], []) for _ in range(N_MIXERS)]
    new_s = [([], []) for _ in range(N_MIXERS)]
    for i in range(depth):
        kind = i % N_MIXERS
        j = i // N_MIXERS
        last = i == depth - 1
        qscale = DIFF_DIM ** -0.5 if kind == 0 else HEAD_DIM ** -0.5
        qscale_p = qscale * LOG2E if kind != 1 else qscale
        qscale_s = qscale * LOG2E if kind == 2 else qscale
        qp, kp, vp, kpb, vpb, sgp = _in_proj(xp, norm_g[i], w_in_bf[i], qscale_p)
        qs, ks, vs, ksb, vsb, sgs = _in_proj(xs, norm_g[i], w_in_bf[i], qscale_s)
        seq_p = lambda a: a.reshape(bp, t, e)
        seq_s = lambda a: a.reshape(bs, n, e)
        cache_k, cache_v = caches[kind]
        if kind == 0:
            lam_init = 0.8 - 0.6 * math.exp(-0.3 * i)
            head_major = lambda a: heads5(a, bp).transpose(0, 2, 3, 1)
            op_t = _diff_prompt(head_major(qp), seq_p(kpb), head_major(vpb), slopes * LOG2E, a_lambda[j],
                                a_subln_g[j], lam_init)
            op = op_t.transpose(0, 3, 1, 2)
            os_ = _diff_sample(seq_s(qs), seq_s(ksb), seq_s(vsb), cache_k, cache_v, j, slopes,
                               a_lambda[j], a_subln_g[j], lam_init)
            new_p[0][0].append(heads5(kp, bp))
            new_p[0][1].append(heads5(vp, bp))
            new_s[0][0].append(heads5(ks, bs))
            new_s[0][1].append(heads5(vs, bs))
        elif kind == 1:
            tq = min(BAND_QROWS, t)
            span = BAND_ROWS + tq
            q_loc = jnp.arange(tq, dtype=jnp.int32)
            k_loc = jnp.arange(span, dtype=jnp.int32) - BAND_ROWS
            qc, kc = q_loc // CHUNK, k_loc // CHUNK
            in_band = (kc[None, :] <= qc[:, None]) & (kc[None, :] >= qc[:, None] - BAND_CHUNKS)
            bias_p = jnp.where(in_band[None], _band_bias(b_rel_bias[j], q_loc, k_loc), NEG_BIG)
            pad = ((0, 0), (BAND_ROWS, 0), (0, 0))
            op = _band_prompt(seq_p(qp), jnp.pad(seq_p(kpb), pad), jnp.pad(seq_p(vpb), pad), bias_p)
            pos_s = past + jnp.arange(n, dtype=jnp.int32)
            bias_c = _band_bias(b_rel_bias[j], pos_s, past - keep_s + jnp.arange(keep_s, dtype=jnp.int32))
            bias_n = _band_bias(b_rel_bias[j], pos_s, pos_s)
            os_ = _band_sample(seq_s(qs), seq_s(ksb), seq_s(vsb), cache_k, cache_v, j, bias_c, bias_n, past)
            new_p[1][0].append(heads5(kp, bp)[:, t - keep_p:])
            new_p[1][1].append(heads5(vp, bp)[:, t - keep_p:])
            new_s[1][0].append(jnp.concatenate([cache_b_k[j], heads5(ks, bs)], axis=1)[:, n:])
            new_s[1][1].append(jnp.concatenate([cache_b_v[j], heads5(vs, bs)], axis=1)[:, n:])
        else:
            op = _stick_prompt(seq_p(qp), seq_p(kpb), seq_p(vpb))
            os_ = _stick_sample(seq_s(qs), seq_s(ksb), seq_s(vsb), cache_k, cache_v, j)
            new_p[2][0].append(heads5(kp, bp))
            new_p[2][1].append(heads5(vp, bp))
            new_s[2][0].append(heads5(ks, bs))
            new_s[2][1].append(heads5(vs, bs))
        xp = _out_proj(op.reshape(bp * t, e), sgp, w_out_bf[i], xp, final_g, last)
        xs = _out_proj(os_.reshape(bs * n, e), sgs, w_out_bf[i], xs, final_g, last)

    outs = [xp.reshape(bp, t, d), xs.reshape(bs, n, d)]
    for group in (new_p, new_s):
        for ks_list, vs_list in group:
            outs.append(jnp.stack(ks_list))
            outs.append(jnp.stack(vs_list))
    return tuple(outs)
```

```python
import functools
import math

import jax
import jax.numpy as jnp
from jax import lax
from jax.experimental import pallas as pl
from jax.experimental.pallas import tpu as pltpu

F32 = jnp.float32
BF16 = jnp.bfloat16

CHUNK = 64
N_MIXERS = 3
N_HEADS = 16
HEAD_DIM = 128
DIFF_DIM = HEAD_DIM // 2
BAND_CHUNKS = 8
BAND_ROWS = BAND_CHUNKS * CHUNK
REL_CLIP = 128
EPS = 1e-6
SUBLN_EPS = 1e-5

VMEM_LIMIT_BYTES = 56 * 1024 * 1024
NEG_BIG = -1e30

IN_PROJ_ROWS = 256
OUT_PROJ_ROWS = 512
DIFF_QROWS = 1024
DIFF_KROWS = 512
ONES_ROWS = 16
DIFF_DEAD = -160.0
STICK_DEAD = -160.0
LOG2E = math.log2(math.e)
BAND_QROWS = 256
STICK_QROWS = 1024
STICK_KROWS = 256


def _params(*sem):
    return pltpu.CompilerParams(dimension_semantics=sem, vmem_limit_bytes=VMEM_LIMIT_BYTES)


def _dot(a, b):
    return jnp.dot(a, b, preferred_element_type=F32)


def _dot_nt(a, b):
    return lax.dot_general(a, b, (((1,), (1,)), ((), ())), preferred_element_type=F32)


def _iota2(shape, dim):
    return lax.broadcasted_iota(jnp.int32, shape, dim)


def _in_proj_kernel(x_ref, g_ref, w_ref, q_ref, k_ref, v_ref, kb_ref, vb_ref, sg_ref, *, qscale, width):
    x = x_ref[...]
    ms = jnp.mean(x * x, axis=-1, keepdims=True)
    xn = (x * lax.rsqrt(ms + EPS) * g_ref[...]).astype(BF16)
    q = _dot(xn, w_ref[:, 0:width])
    q_ref[...] = (q * qscale).astype(BF16)
    k = _dot(xn, w_ref[:, width:2 * width])
    k_ref[...] = k
    kb_ref[...] = k.astype(BF16)
    v = _dot(xn, w_ref[:, 2 * width:3 * width])
    v_ref[...] = v
    vb_ref[...] = v.astype(BF16)
    gate = _dot(xn, w_ref[:, 3 * width:4 * width])
    sg_ref[...] = (gate * jax.nn.sigmoid(gate)).astype(BF16)


def _in_proj(x2d, g, w_bf, qscale):
    m, d = x2d.shape
    width = w_bf.shape[1] // 4
    tm = min(IN_PROJ_ROWS, m)
    row = lambda i: (i, 0)
    fixed = lambda i: (0, 0)
    out_block = pl.BlockSpec((tm, width), row)
    return pl.pallas_call(
        functools.partial(_in_proj_kernel, qscale=qscale, width=width),
        grid=(m // tm,),
        in_specs=[pl.BlockSpec((tm, d), row),
                  pl.BlockSpec((1, d), fixed),
                  pl.BlockSpec((d, 4 * width), fixed, pipeline_mode=pl.Buffered(1))],
        out_specs=[out_block] * 6,
        out_shape=[jax.ShapeDtypeStruct((m, width), BF16),
                   jax.ShapeDtypeStruct((m, width), F32),
                   jax.ShapeDtypeStruct((m, width), F32),
                   jax.ShapeDtypeStruct((m, width), BF16),
                   jax.ShapeDtypeStruct((m, width), BF16),
                   jax.ShapeDtypeStruct((m, width), BF16)],
        compiler_params=_params("arbitrary"),
        name="in_proj",
    )(x2d, g.reshape(1, d), w_bf)


def _out_proj_kernel(o_ref, sg_ref, w_ref, x_ref, fg_ref, y_ref, *, final):
    a = o_ref[...] * sg_ref[...]
    y = x_ref[...] + _dot(a, w_ref[...])
    if final:
        ms = jnp.mean(y * y, axis=-1, keepdims=True)
        y = y * lax.rsqrt(ms + EPS) * fg_ref[...]
    y_ref[...] = y


def _out_proj(o2d, sg2d, w_bf, x2d, final_g, final):
    m, e = o2d.shape
    d = x2d.shape[1]
    tm = min(OUT_PROJ_ROWS, m)
    row = lambda i: (i, 0)
    fixed = lambda i: (0, 0)
    return pl.pallas_call(
        functools.partial(_out_proj_kernel, final=final),
        grid=(m // tm,),
        in_specs=[pl.BlockSpec((tm, e), row),
                  pl.BlockSpec((tm, e), row),
                  pl.BlockSpec((e, d), fixed),
                  pl.BlockSpec((tm, d), row),
                  pl.BlockSpec((1, d), fixed)],
        out_specs=pl.BlockSpec((tm, d), row),
        out_shape=jax.ShapeDtypeStruct((m, d), F32),
        compiler_params=_params("arbitrary"),
        name="out_proj_final" if final else "out_proj",
    )(o2d, sg2d, w_bf, x2d, final_g.reshape(1, d))


def _split_q(q):
    lane = _iota2(q.shape, 1)
    zero = jnp.zeros_like(q)
    return jnp.where(lane < DIFF_DIM, q, zero), jnp.where(lane >= DIFF_DIM, q, zero)


def _diff_lambda(lam_ref, lam_init):
    lp = lam_ref[0]
    e1 = jnp.exp(jnp.sum(lp[0:1, :] * lp[1:2, :], axis=1, keepdims=True))
    e2 = jnp.exp(jnp.sum(lp[2:3, :] * lp[3:4, :], axis=1, keepdims=True))
    return e1 - e2 + lam_init


def _diff_finish(o1, l1, o2, l2, lam, g, lam_init):
    o = o1 / l1 - lam * (o2 / l2)
    ms = jnp.mean(o * o, axis=-1, keepdims=True)
    return o * lax.rsqrt(ms + SUBLN_EPS) * g * (1.0 - lam_init)


def _diff_prompt_kernel(slope_ref, lam_ref, g_ref, qt_ref, k_ref, vt_ref, o_ref, acc_ref, kn_ref, *,
                        tq, tk, lam_init):
    h = pl.program_id(1)
    qb = pl.program_id(2)
    slope2 = slope_ref[h]
    t_keys = k_ref.shape[1]
    lanes = 2 * tq
    nsub = tq // tk
    step = slope2 * tk

    @pl.when(qb == 0)
    def _():
        same_map = (_iota2((HEAD_DIM, HEAD_DIM), 0) < DIFF_DIM) == (_iota2((HEAD_DIM, HEAD_DIM), 1) < DIFF_DIM)
        sel = jnp.where(same_map, 1.0, 0.0).astype(BF16)

        def body(i, mx):
            kb = k_ref[0, pl.ds(pl.multiple_of(i * tk, tk), tk), :].astype(F32)
            norm2 = _dot((kb * kb).astype(BF16), sel)
            return jnp.maximum(mx, jnp.max(norm2, axis=0, keepdims=True))
        mx = lax.fori_loop(0, t_keys // tk, body, jnp.zeros((1, HEAD_DIM), F32))
        kn_ref[...] = jnp.sqrt(mx * 1.01)

    qt = qt_ref[0, 0]
    row = _iota2(qt.shape, 0)
    zero = jnp.zeros_like(qt)
    q2t = jnp.concatenate([jnp.where(row < DIFF_DIM, qt, zero), jnp.where(row >= DIFF_DIM, qt, zero)], axis=1)
    qf = q2t.astype(F32)
    qn = jnp.sqrt(jnp.sum(qf * qf, axis=0, keepdims=True))
    kn = jnp.where(_iota2((1, lanes), 1) < tq, kn_ref[0:1, 0:1], kn_ref[0:1, DIFF_DIM:DIFF_DIM + 1])
    bound = qn * kn * 1.001 + 1.0 + slope2 * (tk - 1)

    acc_ref[...] = jnp.zeros_like(acc_ref)

    def visit(k_start, bias, m, lo, hi):
        kblk = k_ref[0, pl.ds(k_start, tk), :]
        vtblk = jnp.concatenate([vt_ref[0, 0, :, pl.ds(k_start, tk)], jnp.ones((ONES_ROWS, tk), BF16)], axis=0)
        if bias is None:
            st = _dot(jnp.concatenate([kblk, key_bias], axis=1), q2t_ones[:, lo:hi])
        else:
            st = _dot(kblk, q2t[:, lo:hi]) + bias
        m_new = jnp.maximum(m, jnp.max(st, axis=0, keepdims=True))
        alpha = jnp.exp2(m - m_new)
        p = jnp.exp2(st - m_new).astype(BF16)
        acc_ref[:, lo:hi] = alpha * acc_ref[:, lo:hi] + _dot(vtblk, p)
        return m_new

    jl = _iota2((tk, tq), 0)
    iq = _iota2((tk, tq), 1)
    rel = (iq - jl).astype(F32)
    diag_bias = jnp.where((jl // CHUNK) <= (iq // CHUNK), slope2 * ((rel - jnp.abs(rel)) + jl.astype(F32)),
                          2.0 * NEG_BIG)
    m_maps = [jnp.full((1, tq), NEG_BIG, F32), jnp.full((1, tq), NEG_BIG, F32)]
    q0 = pl.multiple_of(qb * tq, tq)
    for sub in reversed(range(nsub)):
        off = sub * tk
        for mp in range(2):
            seen = visit(q0 + off, diag_bias[:, :tq - off], m_maps[mp][:, off:] + step, mp * tq + off, (mp + 1) * tq)
            m_maps[mp] = jnp.concatenate([m_maps[mp][:, :off] + step, seen], axis=1) if off else seen
    m = jnp.concatenate(m_maps, axis=1)

    in_block = slope2 * _iota2((tk, HEAD_DIM), 0).astype(F32)
    col = _iota2((tk, HEAD_DIM), 1)
    part0 = in_block.astype(BF16).astype(F32)
    part1 = (in_block - part0).astype(BF16).astype(F32)
    part2 = in_block - part0 - part1
    key_bias = jnp.where(col == 0, part0, jnp.where(col == 1, part1, jnp.where(col == 2, part2, 0.0))).astype(BF16)
    ones_rows = jnp.where(_iota2((HEAD_DIM, lanes), 0) < 3, 1.0, 0.0).astype(BF16)
    q2t_ones = jnp.concatenate([q2t, ones_rows], axis=0)

    def cond(state):
        j, worst, _ = state
        return (j >= 0) & (worst > DIFF_DEAD)

    def body(state):
        j, _, m = state
        m = visit(pl.multiple_of(j * tk, tk), None, m + step, 0, lanes)
        return j - 1, jnp.max(bound - (m + step)), m

    lax.while_loop(cond, body, (qb * nsub - 1, jnp.max(bound - (m + step)), m))

    acc = acc_ref[...]
    norm = acc[HEAD_DIM:HEAD_DIM + 1, :]
    lam = _diff_lambda(lam_ref, lam_init)
    o = acc[:HEAD_DIM, :tq] / norm[:, :tq] - lam * (acc[:HEAD_DIM, tq:] / norm[:, tq:])
    ms = jnp.mean(o * o, axis=0, keepdims=True)
    o = o * lax.rsqrt(ms + SUBLN_EPS) * g_ref[...] * (1.0 - lam_init)
    o_ref[0, 0] = o.astype(o_ref.dtype)


def _diff_prompt(qt, k, vt, slopes2, lam_p, subln_g, lam_init):
    b, _, _, t = qt.shape
    tq = DIFF_QROWS
    tk = DIFF_KROWS
    assert t % tq == 0 and tq % tk == 0 and tk % CHUNK == 0
    vrows = HEAD_DIM + ONES_ROWS
    return pl.pallas_call(
        functools.partial(_diff_prompt_kernel, tq=tq, tk=tk, lam_init=lam_init),
        grid=(b, N_HEADS, t // tq),
        in_specs=[pl.BlockSpec(memory_space=pltpu.SMEM),
                  pl.BlockSpec((1, 4, DIFF_DIM), lambda bi, hi, qi: (0, 0, 0)),
                  pl.BlockSpec((HEAD_DIM, 1), lambda bi, hi, qi: (0, 0)),
                  pl.BlockSpec((1, 1, HEAD_DIM, tq), lambda bi, hi, qi: (bi, hi, 0, qi)),
                  pl.BlockSpec((1, t, HEAD_DIM), lambda bi, hi, qi: (bi, 0, hi)),
                  pl.BlockSpec((1, 1, HEAD_DIM, t), lambda bi, hi, qi: (bi, hi, 0, 0))],
        out_specs=pl.BlockSpec((1, 1, HEAD_DIM, tq), lambda bi, hi, qi: (bi, hi, 0, qi)),
        out_shape=jax.ShapeDtypeStruct((b, N_HEADS, HEAD_DIM, t), BF16),
        scratch_shapes=[pltpu.VMEM((vrows, 2 * tq), F32), pltpu.VMEM((1, HEAD_DIM), F32)],
        compiler_params=_params("arbitrary", "arbitrary", "arbitrary"),
        name="diff_prompt",
    )(slopes2, lam_p.reshape(1, 4, DIFF_DIM), subln_g.reshape(HEAD_DIM, 1), qt, k, vt)


def _softmax_parts(scores, values):
    m = functools.reduce(jnp.maximum, [jnp.max(s, axis=-1, keepdims=True) for s in scores])
    ps = [jnp.exp(s - m) for s in scores]
    l = sum(jnp.sum(p, axis=-1, keepdims=True) for p in ps)
    o = sum(_dot(p.astype(BF16), v) for p, v in zip(ps, values))
    return o, l


def _diff_sample_kernel(slope_ref, lam_ref, g_ref, q_ref, kn_ref, vn_ref, kc_ref, vc_ref, o_ref, *,
                        past, lam_init):
    h = pl.program_id(1)
    slope = slope_ref[h]
    q1, q2 = _split_q(q_ref[0])
    n = q1.shape[0]
    kc = kc_ref[0, 0].astype(BF16)
    vc = vc_ref[0, 0].astype(BF16)
    kn = kn_ref[0]
    vn = vn_ref[0]

    def bias(k_start, k_len):
        q_pos = past + _iota2((n, k_len), 0)
        k_pos = k_start + _iota2((n, k_len), 1)
        allowed = (k_pos // CHUNK) <= (q_pos // CHUNK)
        return jnp.where(allowed, -slope * jnp.abs(q_pos - k_pos).astype(F32), NEG_BIG)

    bias_c = bias(0, kc.shape[0])
    bias_n = bias(past, n)
    o1, l1 = _softmax_parts([_dot_nt(q1, kc) + bias_c, _dot_nt(q1, kn) + bias_n], [vc, vn])
    o2, l2 = _softmax_parts([_dot_nt(q2, kc) + bias_c, _dot_nt(q2, kn) + bias_n], [vc, vn])
    lam = _diff_lambda(lam_ref, lam_init)
    o_ref[0] = _diff_finish(o1, l1, o2, l2, lam, g_ref[...], lam_init).astype(o_ref.dtype)


def _sample_specs(n, past, layer):
    new_rows = lambda bi, hi: (bi, 0, hi)
    cache_rows = lambda bi, hi: (layer, bi, 0, hi)
    new = pl.BlockSpec((1, n, HEAD_DIM), new_rows)
    cache = pl.BlockSpec((1, 1, past, HEAD_DIM), cache_rows)
    return new, cache


def _diff_sample(q, k_new, v_new, cache_k, cache_v, layer, slopes, lam_p, subln_g, lam_init):
    b, n, e = q.shape
    past = cache_k.shape[2]
    new, cache = _sample_specs(n, past, layer)
    return pl.pallas_call(
        functools.partial(_diff_sample_kernel, past=past, lam_init=lam_init),
        grid=(b, N_HEADS),
        in_specs=[pl.BlockSpec(memory_space=pltpu.SMEM),
                  pl.BlockSpec((1, 4, DIFF_DIM), lambda bi, hi: (0, 0, 0)),
                  pl.BlockSpec((1, HEAD_DIM), lambda bi, hi: (0, 0)),
                  new, new, new, cache, cache],
        out_specs=new,
        out_shape=jax.ShapeDtypeStruct((b, n, e), BF16),
        compiler_params=_params("arbitrary", "arbitrary"),
        name="diff_sample",
    )(slopes, lam_p.reshape(1, 4, DIFF_DIM), subln_g.reshape(1, HEAD_DIM), q, k_new, v_new, cache_k, cache_v)


def _band_prompt_kernel(bias_ref, q_ref, k_ref, v_ref, o_ref, *, tq, span):
    q0 = pl.program_id(2) * tq
    back = jnp.minimum(q0, BAND_ROWS)
    start = pl.multiple_of(q0 - back, tq)
    shift = pl.multiple_of(BAND_ROWS - back, tq)
    k = k_ref[0, pl.ds(start, span), :]
    v = v_ref[0, pl.ds(start, span), :]
    s = _dot_nt(q_ref[0], k) + bias_ref[0, :, pl.ds(shift, span)]
    o, l = _softmax_parts([s], [v])
    o_ref[0] = (o / l).astype(o_ref.dtype)


def _band_prompt(q, k, v, bias):
    b, t, e = q.shape
    tq = bias.shape[1]
    wide = bias.shape[2]
    span = wide - BAND_ROWS
    assert t % tq == 0 and t >= span and BAND_ROWS % tq == 0
    head_rows = lambda bi, hi, qi: (bi, 0, hi)
    q_rows = lambda bi, hi, qi: (bi, qi, hi)
    return pl.pallas_call(
        functools.partial(_band_prompt_kernel, tq=tq, span=span),
        grid=(b, N_HEADS, t // tq),
        in_specs=[pl.BlockSpec((1, tq, wide), lambda bi, hi, qi: (hi, 0, 0)),
                  pl.BlockSpec((1, tq, HEAD_DIM), q_rows),
                  pl.BlockSpec((1, t, HEAD_DIM), head_rows),
                  pl.BlockSpec((1, t, HEAD_DIM), head_rows)],
        out_specs=pl.BlockSpec((1, tq, HEAD_DIM), q_rows),
        out_shape=jax.ShapeDtypeStruct((b, t, e), BF16),
        compiler_params=_params("arbitrary", "arbitrary", "arbitrary"),
        name="band_prompt",
    )(bias, q, k, v)


def _band_prompt_bias(rel_bias, tq):
    wide = 2 * BAND_ROWS + tq
    c0 = wide - 1 - BAND_ROWS
    n = tq + wide - 1
    lead = c0 - REL_CLIP
    rb = rel_bias.astype(F32)
    h = rb.shape[0]
    v = jnp.concatenate([jnp.broadcast_to(rb[:, :1], (h, lead)), rb,
                         jnp.broadcast_to(rb[:, -1:], (h, n - lead - rb.shape[1]))], axis=1)
    x = jnp.tile(v, (1, tq + 1))[:, :tq * (n + 1)].reshape(h, tq, n + 1)[:, :, :wide]
    tile = x[:, :, ::-1]
    i = jnp.arange(tq, dtype=jnp.int32)[:, None] // CHUNK
    kc = jnp.arange(wide, dtype=jnp.int32)[None, :] // CHUNK - BAND_CHUNKS
    in_band = (kc <= i) & (kc >= i - BAND_CHUNKS)
    return jnp.where(in_band[None], tile, NEG_BIG)


def _band_sample_kernel(bc_ref, bn_ref, q_ref, kn_ref, vn_ref, kc_ref, vc_ref, o_ref, *, past):
    q = q_ref[0]
    n = q.shape[0]
    kc = kc_ref[0, 0].astype(BF16)
    vc = vc_ref[0, 0].astype(BF16)
    keep = kc.shape[0]

    def valid(k_start, k_len):
        q_pos = past + _iota2((n, k_len), 0)
        k_pos = k_start + _iota2((n, k_len), 1)
        qc = q_pos // CHUNK
        kc_ = k_pos // CHUNK
        return (k_pos >= 0) & (kc_ <= qc) & (kc_ >= qc - BAND_CHUNKS)

    s_c = jnp.where(valid(past - keep, keep), _dot_nt(q, kc) + bc_ref[0], NEG_BIG)
    s_n = jnp.where(valid(past, n), _dot_nt(q, kn_ref[0]) + bn_ref[0], NEG_BIG)
    o, l = _softmax_parts([s_c, s_n], [vc, vn_ref[0]])
    o_ref[0] = (o / l).astype(o_ref.dtype)


def _band_sample(q, k_new, v_new, cache_k, cache_v, layer, bias_c, bias_n, past):
    b, n, e = q.shape
    keep = cache_k.shape[2]
    new, cache = _sample_specs(n, keep, layer)
    return pl.pallas_call(
        functools.partial(_band_sample_kernel, past=past),
        grid=(b, N_HEADS),
        in_specs=[pl.BlockSpec((1, n, keep), lambda bi, hi: (hi, 0, 0)),
                  pl.BlockSpec((1, n, n), lambda bi, hi: (hi, 0, 0)),
                  new, new, new, cache, cache],
        out_specs=new,
        out_shape=jax.ShapeDtypeStruct((b, n, e), BF16),
        compiler_params=_params("arbitrary", "arbitrary"),
        name="band_sample",
    )(bias_c, bias_n, q, k_new, v_new, cache_k, cache_v)


def _band_bias(rel_bias, q_pos, k_pos):
    idx = jnp.clip(q_pos[:, None] - k_pos[None, :], -REL_CLIP, REL_CLIP) + REL_CLIP
    return rel_bias.astype(F32)[:, idx]


def _strict_upper(nk):
    return jnp.where(_iota2((nk, nk), 0) > _iota2((nk, nk), 1), 1.0, 0.0).astype(BF16)


def _stick_block(q, k, v, upper, carry, valid):
    z = _dot_nt(q, k)
    soft = jnp.log2(1.0 + jnp.exp2(-jnp.abs(z)))
    log_beta = jnp.minimum(z, 0.0) - soft
    log_keep = -jnp.maximum(z, 0.0) - soft
    if valid is not None:
        log_keep = jnp.where(valid, log_keep, 0.0)
    hi = log_keep.astype(BF16)
    lo = (log_keep - hi.astype(F32)).astype(BF16)
    after = _dot(hi, upper) + _dot(lo, upper) + carry
    w = jnp.exp2(log_beta + after)
    if valid is not None:
        w = jnp.where(valid, w, 0.0)
    out = _dot(w.astype(BF16), v)
    return out, carry + jnp.sum(log_keep, axis=-1, keepdims=True)


def _stick_prompt_kernel(q_ref, k_ref, v_ref, o_ref, acc_ref, carry_ref, *, tq, tk):
    qb = pl.program_id(2)
    upper = _strict_upper(tk)
    q0 = pl.multiple_of(qb * tq, tq)
    acc_ref[...] = jnp.zeros_like(acc_ref)
    carry_ref[...] = jnp.zeros_like(carry_ref)

    def fold(k_start, r0, valid):
        out, carry = _stick_block(q_ref[0, r0:, :], k_ref[0, pl.ds(k_start, tk), :], v_ref[0, pl.ds(k_start, tk), :],
                                  upper, carry_ref[r0:, :], valid)
        acc_ref[r0:, :] += out
        carry_ref[r0:, :] = carry
        return carry

    for r0 in reversed(range(0, tq, tk)):
        rows = tq - r0
        fold(q0 + r0, r0, _iota2((rows, tk), 1) < _iota2((rows, tk), 0))

    def cond(state):
        j, alive = state
        return (j >= 0) & (alive > STICK_DEAD)

    def body(state):
        j, _ = state
        carry = fold(pl.multiple_of(j * tk, tk), 0, None)
        return j - 1, jnp.max(carry)

    lax.while_loop(cond, body, (qb * (tq // tk) - 1, jnp.max(carry_ref[...])))
    o_ref[0] = acc_ref[...].astype(o_ref.dtype)


def _stick_prompt(q, k, v):
    b, t, e = q.shape
    tq = STICK_QROWS
    tk = STICK_KROWS
    assert t % tq == 0 and tq % tk == 0
    head_rows = lambda bi, hi, qi: (bi, 0, hi)
    q_rows = lambda bi, hi, qi: (bi, qi, hi)
    return pl.pallas_call(
        functools.partial(_stick_prompt_kernel, tq=tq, tk=tk),
        grid=(b, N_HEADS, t // tq),
        in_specs=[pl.BlockSpec((1, tq, HEAD_DIM), q_rows),
                  pl.BlockSpec((1, t, HEAD_DIM), head_rows),
                  pl.BlockSpec((1, t, HEAD_DIM), head_rows)],
        out_specs=pl.BlockSpec((1, tq, HEAD_DIM), q_rows),
        out_shape=jax.ShapeDtypeStruct((b, t, e), BF16),
        scratch_shapes=[pltpu.VMEM((tq, HEAD_DIM), F32), pltpu.VMEM((tq, 1), F32)],
        compiler_params=_params("arbitrary", "arbitrary", "arbitrary"),
        name="stick_prompt",
    )(q, k, v)


def _stick_sample_kernel(q_ref, kn_ref, vn_ref, kc_ref, vc_ref, o_ref, *, blk):
    q = q_ref[0]
    n = q.shape[0]
    past = kc_ref.shape[2]
    upper = _strict_upper(blk)
    causal = _iota2((n, n), 1) < _iota2((n, n), 0)
    acc, carry = _stick_block(q, kn_ref[0], vn_ref[0], upper[:n, :n], jnp.zeros((n, 1), F32), causal)
    for j in reversed(range(past // blk)):
        kc = kc_ref[0, 0, j * blk:(j + 1) * blk, :].astype(BF16)
        vc = vc_ref[0, 0, j * blk:(j + 1) * blk, :].astype(BF16)
        out, carry = _stick_block(q, kc, vc, upper, carry, None)
        acc = acc + out
    o_ref[0] = acc.astype(o_ref.dtype)


def _stick_sample(q, k_new, v_new, cache_k, cache_v, layer):
    b, n, e = q.shape
    past = cache_k.shape[2]
    blk = min(256, past)
    assert past % blk == 0
    new, cache = _sample_specs(n, past, layer)
    return pl.pallas_call(
        functools.partial(_stick_sample_kernel, blk=blk),
        grid=(b, N_HEADS),
        in_specs=[new, new, new, cache, cache],
        out_specs=new,
        out_shape=jax.ShapeDtypeStruct((b, n, e), BF16),
        compiler_params=_params("arbitrary", "arbitrary"),
        name="stick_sample",
    )(q, k_new, v_new, cache_k, cache_v)


def kernel(x_prompt, x_sample, cache_a_k, cache_a_v, cache_b_k, cache_b_v, cache_c_k, cache_c_v,
           norm_g, w_in, w_out, a_lambda, a_subln_g, b_rel_bias, final_g):
    bp, t, d = x_prompt.shape
    bs, n, _ = x_sample.shape
    depth = w_in.shape[0]
    e = w_out.shape[1]
    past = cache_a_k.shape[2]
    keep_s = cache_b_k.shape[2]
    keep_p = min(BAND_ROWS, t)
    heads5 = lambda a, b_: a.reshape(b_, -1, N_HEADS, HEAD_DIM)
    flat_cache = lambda c: c.reshape(c.shape[0], c.shape[1], c.shape[2], e)

    slopes = 2.0 ** (-8.0 * jnp.arange(1, N_HEADS + 1, dtype=F32) / N_HEADS)
    w_in_bf = w_in.astype(BF16)
    w_out_bf = w_out.astype(BF16)
    caches = [(flat_cache(cache_a_k), flat_cache(cache_a_v)),
              (flat_cache(cache_b_k), flat_cache(cache_b_v)),
              (flat_cache(cache_c_k), flat_cache(cache_c_v))]

    xp = x_prompt.reshape(bp * t, d)
    xs = x_sample.reshape(bs * n, d)
    new_p = [([], []) for _ in range(N_MIXERS)]
    new_s = [([], []) for _ in range(N_MIXERS)]
    for i in range(depth):
        kind = i % N_MIXERS
        j = i // N_MIXERS
        last = i == depth - 1
        qscale = DIFF_DIM ** -0.5 if kind == 0 else HEAD_DIM ** -0.5
        qscale_p = qscale * LOG2E if kind != 1 else qscale
        qscale_s = qscale * LOG2E if kind == 2 else qscale
        qp, kp, vp, kpb, vpb, sgp = _in_proj(xp, norm_g[i], w_in_bf[i], qscale_p)
        qs, ks, vs, ksb, vsb, sgs = _in_proj(xs, norm_g[i], w_in_bf[i], qscale_s)
        seq_p = lambda a: a.reshape(bp, t, e)
        seq_s = lambda a: a.reshape(bs, n, e)
        cache_k, cache_v = caches[kind]
        if kind == 0:
            lam_init = 0.8 - 0.6 * math.exp(-0.3 * i)
            head_major = lambda a: heads5(a, bp).transpose(0, 2, 3, 1)
            op_t = _diff_prompt(head_major(qp), seq_p(kpb), head_major(vpb), slopes * LOG2E, a_lambda[j],
                                a_subln_g[j], lam_init)
            op = op_t.transpose(0, 3, 1, 2)
            os_ = _diff_sample(seq_s(qs), seq_s(ksb), seq_s(vsb), cache_k, cache_v, j, slopes,
                               a_lambda[j], a_subln_g[j], lam_init)
            new_p[0][0].append(heads5(kp, bp))
            new_p[0][1].append(heads5(vp, bp))
            new_s[0][0].append(heads5(ks, bs))
            new_s[0][1].append(heads5(vs, bs))
        elif kind == 1:
            op = _band_prompt(seq_p(qp), seq_p(kpb), seq_p(vpb), _band_prompt_bias(b_rel_bias[j], BAND_QROWS))
            pos_s = past + jnp.arange(n, dtype=jnp.int32)
            bias_c = _band_bias(b_rel_bias[j], pos_s, past - keep_s + jnp.arange(keep_s, dtype=jnp.int32))
            bias_n = _band_bias(b_rel_bias[j], pos_s, pos_s)
            os_ = _band_sample(seq_s(qs), seq_s(ksb), seq_s(vsb), cache_k, cache_v, j, bias_c, bias_n, past)
            new_p[1][0].append(heads5(kp, bp)[:, t - keep_p:])
            new_p[1][1].append(heads5(vp, bp)[:, t - keep_p:])
            new_s[1][0].append(jnp.concatenate([cache_b_k[j], heads5(ks, bs)], axis=1)[:, n:])
            new_s[1][1].append(jnp.concatenate([cache_b_v[j], heads5(vs, bs)], axis=1)[:, n:])
        else:
            op = _stick_prompt(seq_p(qp), seq_p(kpb), seq_p(vpb))
            os_ = _stick_sample(seq_s(qs), seq_s(ksb), seq_s(vsb), cache_k, cache_v, j)
            new_p[2][0].append(heads5(kp, bp))
            new_p[2][1].append(heads5(vp, bp))
            new_s[2][0].append(heads5(ks, bs))
            new_s[2][1].append(heads5(vs, bs))
        xp = _out_proj(op.reshape(bp * t, e), sgp, w_out_bf[i], xp, final_g, last)
        xs = _out_proj(os_.reshape(bs * n, e), sgs, w_out_bf[i], xs, final_g, last)

    outs = [xp.reshape(bp, t, d), xs.reshape(bs, n, d)]
    for group in (new_p, new_s):
        for ks_list, vs_list in group:
            outs.append(jnp.stack(ks_list))
            outs.append(jnp.stack(vs_list))
    return tuple(outs)
```

```python
import functools
import math

import jax
import jax.numpy as jnp
from jax import lax
from jax.experimental import pallas as pl
from jax.experimental.pallas import tpu as pltpu

F32 = jnp.float32
BF16 = jnp.bfloat16

CHUNK = 64
N_MIXERS = 3
N_HEADS = 16
HEAD_DIM = 128
DIFF_DIM = HEAD_DIM // 2
BAND_CHUNKS = 8
BAND_ROWS = BAND_CHUNKS * CHUNK
REL_CLIP = 128
EPS = 1e-6
SUBLN_EPS = 1e-5

VMEM_LIMIT_BYTES = 56 * 1024 * 1024
NEG_BIG = -1e30

IN_PROJ_ROWS = 256
OUT_PROJ_ROWS = 512
DIFF_QROWS = 1024
DIFF_KROWS = 512
ONES_ROWS = 16
DIFF_DEAD = -160.0
STICK_DEAD = -160.0
LOG2E = math.log2(math.e)
BAND_QROWS = 256
STICK_QROWS = 1024
STICK_KROWS = 256


def _params(*sem):
    return pltpu.CompilerParams(dimension_semantics=sem, vmem_limit_bytes=VMEM_LIMIT_BYTES)


def _dot(a, b):
    return jnp.dot(a, b, preferred_element_type=F32)


def _dot_nt(a, b):
    return lax.dot_general(a, b, (((1,), (1,)), ((), ())), preferred_element_type=F32)


def _iota2(shape, dim):
    return lax.broadcasted_iota(jnp.int32, shape, dim)


def _in_proj_kernel(x_ref, g_ref, w_ref, q_ref, k_ref, v_ref, kb_ref, vb_ref, sg_ref, *, qscale, width):
    x = x_ref[...]
    ms = jnp.mean(x * x, axis=-1, keepdims=True)
    xn = (x * lax.rsqrt(ms + EPS) * g_ref[...]).astype(BF16)
    q = _dot(xn, w_ref[:, 0:width])
    q_ref[...] = (q * qscale).astype(BF16)
    k = _dot(xn, w_ref[:, width:2 * width])
    k_ref[...] = k
    kb_ref[...] = k.astype(BF16)
    v = _dot(xn, w_ref[:, 2 * width:3 * width])
    v_ref[...] = v
    vb_ref[...] = v.astype(BF16)
    gate = _dot(xn, w_ref[:, 3 * width:4 * width])
    sg_ref[...] = (gate * jax.nn.sigmoid(gate)).astype(BF16)


def _in_proj(x2d, g, w_bf, qscale):
    m, d = x2d.shape
    width = w_bf.shape[1] // 4
    tm = min(IN_PROJ_ROWS, m)
    row = lambda i: (i, 0)
    fixed = lambda i: (0, 0)
    out_block = pl.BlockSpec((tm, width), row)
    return pl.pallas_call(
        functools.partial(_in_proj_kernel, qscale=qscale, width=width),
        grid=(m // tm,),
        in_specs=[pl.BlockSpec((tm, d), row),
                  pl.BlockSpec((1, d), fixed),
                  pl.BlockSpec((d, 4 * width), fixed, pipeline_mode=pl.Buffered(1))],
        out_specs=[out_block] * 6,
        out_shape=[jax.ShapeDtypeStruct((m, width), BF16),
                   jax.ShapeDtypeStruct((m, width), F32),
                   jax.ShapeDtypeStruct((m, width), F32),
                   jax.ShapeDtypeStruct((m, width), BF16),
                   jax.ShapeDtypeStruct((m, width), BF16),
                   jax.ShapeDtypeStruct((m, width), BF16)],
        compiler_params=_params("arbitrary"),
        name="in_proj",
    )(x2d, g.reshape(1, d), w_bf)


def _out_proj_kernel(o_ref, sg_ref, w_ref, x_ref, fg_ref, y_ref, *, final):
    a = o_ref[...] * sg_ref[...]
    y = x_ref[...] + _dot(a, w_ref[...])
    if final:
        ms = jnp.mean(y * y, axis=-1, keepdims=True)
        y = y * lax.rsqrt(ms + EPS) * fg_ref[...]
    y_ref[...] = y


def _out_proj(o2d, sg2d, w_bf, x2d, final_g, final):
    m, e = o2d.shape
    d = x2d.shape[1]
    tm = min(OUT_PROJ_ROWS, m)
    row = lambda i: (i, 0)
    fixed = lambda i: (0, 0)
    return pl.pallas_call(
        functools.partial(_out_proj_kernel, final=final),
        grid=(m // tm,),
        in_specs=[pl.BlockSpec((tm, e), row),
                  pl.BlockSpec((tm, e), row),
                  pl.BlockSpec((e, d), fixed),
                  pl.BlockSpec((tm, d), row),
                  pl.BlockSpec((1, d), fixed)],
        out_specs=pl.BlockSpec((tm, d), row),
        out_shape=jax.ShapeDtypeStruct((m, d), F32),
        compiler_params=_params("arbitrary"),
        name="out_proj_final" if final else "out_proj",
    )(o2d, sg2d, w_bf, x2d, final_g.reshape(1, d))


def _split_q(q):
    lane = _iota2(q.shape, 1)
    zero = jnp.zeros_like(q)
    return jnp.where(lane < DIFF_DIM, q, zero), jnp.where(lane >= DIFF_DIM, q, zero)


def _diff_lambda(lam_ref, lam_init):
    lp = lam_ref[0]
    e1 = jnp.exp(jnp.sum(lp[0:1, :] * lp[1:2, :], axis=1, keepdims=True))
    e2 = jnp.exp(jnp.sum(lp[2:3, :] * lp[3:4, :], axis=1, keepdims=True))
    return e1 - e2 + lam_init


def _diff_finish(o1, l1, o2, l2, lam, g, lam_init):
    o = o1 / l1 - lam * (o2 / l2)
    ms = jnp.mean(o * o, axis=-1, keepdims=True)
    return o * lax.rsqrt(ms + SUBLN_EPS) * g * (1.0 - lam_init)


def _diff_prompt_kernel(slope_ref, lam_ref, g_ref, qt_ref, k_ref, vt_ref, o_ref, acc_ref, kn_ref, *,
                        tq, tk, lam_init):
    h = pl.program_id(1)
    qb = pl.program_id(2)
    slope2 = slope_ref[h]
    t_keys = k_ref.shape[1]
    lanes = 2 * tq
    nsub = tq // tk
    step = slope2 * tk

    @pl.when(qb == 0)
    def _():
        same_map = (_iota2((HEAD_DIM, HEAD_DIM), 0) < DIFF_DIM) == (_iota2((HEAD_DIM, HEAD_DIM), 1) < DIFF_DIM)
        sel = jnp.where(same_map, 1.0, 0.0).astype(BF16)

        def body(i, mx):
            kb = k_ref[0, pl.ds(pl.multiple_of(i * tk, tk), tk), :].astype(F32)
            norm2 = _dot((kb * kb).astype(BF16), sel)
            return jnp.maximum(mx, jnp.max(norm2, axis=0, keepdims=True))
        mx = lax.fori_loop(0, t_keys // tk, body, jnp.zeros((1, HEAD_DIM), F32))
        kn_ref[...] = jnp.sqrt(mx * 1.01)

    qt = qt_ref[0, 0]
    row = _iota2(qt.shape, 0)
    zero = jnp.zeros_like(qt)
    qm = [jnp.where(row < DIFF_DIM, qt, zero), jnp.where(row >= DIFF_DIM, qt, zero)]
    q2t = jnp.concatenate([qm[mp][:, ti * tk:(ti + 1) * tk] for ti in range(nsub) for mp in range(2)], axis=1)
    qf = q2t.astype(F32)
    qn = jnp.sqrt(jnp.sum(qf * qf, axis=0, keepdims=True))
    kn = jnp.where((_iota2((1, lanes), 1) // tk) % 2 == 0, kn_ref[0:1, 0:1], kn_ref[0:1, DIFF_DIM:DIFF_DIM + 1])
    bound = qn * kn * 1.001 + 1.0 + slope2 * (tk - 1)

    acc_ref[...] = jnp.zeros_like(acc_ref)

    def visit(k_start, bias, m, lo, hi):
        kblk = k_ref[0, pl.ds(k_start, tk), :]
        vtblk = jnp.concatenate([vt_ref[0, 0, :, pl.ds(k_start, tk)], jnp.ones((ONES_ROWS, tk), BF16)], axis=0)
        if bias is None:
            st = _dot(jnp.concatenate([kblk, key_bias], axis=1), q2t_ones[:, lo:hi])
        else:
            st = _dot(kblk, q2t[:, lo:hi]) + bias
        m_new = jnp.maximum(m, jnp.max(st, axis=0, keepdims=True))
        alpha = jnp.exp2(m - m_new)
        p = jnp.exp2(st - m_new).astype(BF16)
        acc_ref[:, lo:hi] = alpha * acc_ref[:, lo:hi] + _dot(vtblk, p)
        return m_new

    jl = _iota2((tk, tq), 0)
    iq = _iota2((tk, tq), 1)
    rel = (iq - jl).astype(F32)
    diag_bias = jnp.where((jl // CHUNK) <= (iq // CHUNK), slope2 * ((rel - jnp.abs(rel)) + jl.astype(F32)),
                          2.0 * NEG_BIG)
    m = jnp.full((1, lanes), NEG_BIG, F32)
    q0 = pl.multiple_of(qb * tq, tq)
    for sub in reversed(range(nsub)):
        off = sub * tk
        bias = jnp.concatenate([diag_bias[:, r * tk:(r + 1) * tk] for r in range(nsub - sub) for _ in range(2)], axis=1)
        seen = visit(q0 + off, bias, m[:, 2 * off:] + step, 2 * off, lanes)
        m = jnp.concatenate([m[:, :2 * off] + step, seen], axis=1) if off else seen

    in_block = slope2 * _iota2((tk, HEAD_DIM), 0).astype(F32)
    col = _iota2((tk, HEAD_DIM), 1)
    part0 = in_block.astype(BF16).astype(F32)
    part1 = (in_block - part0).astype(BF16).astype(F32)
    part2 = in_block - part0 - part1
    key_bias = jnp.where(col == 0, part0, jnp.where(col == 1, part1, jnp.where(col == 2, part2, 0.0))).astype(BF16)
    ones_rows = jnp.where(_iota2((HEAD_DIM, lanes), 0) < 3, 1.0, 0.0).astype(BF16)
    q2t_ones = jnp.concatenate([q2t, ones_rows], axis=0)

    def cond(state):
        j, worst, _ = state
        return (j >= 0) & (worst > DIFF_DEAD)

    def body(state):
        j, _, m = state
        m = visit(pl.multiple_of(j * tk, tk), None, m + step, 0, lanes)
        return j - 1, jnp.max(bound - (m + step)), m

    lax.while_loop(cond, body, (qb * nsub - 1, jnp.max(bound - (m + step)), m))

    acc = acc_ref[...]
    norm = acc[HEAD_DIM:HEAD_DIM + 1, :]
    lam = _diff_lambda(lam_ref, lam_init)
    ratio = acc[:HEAD_DIM, :] / norm
    per_map = [jnp.concatenate([ratio[:, (2 * ti + mp) * tk:(2 * ti + mp + 1) * tk] for ti in range(nsub)], axis=1)
               for mp in range(2)]
    o = per_map[0] - lam * per_map[1]
    ms = jnp.mean(o * o, axis=0, keepdims=True)
    o = o * lax.rsqrt(ms + SUBLN_EPS) * g_ref[...] * (1.0 - lam_init)
    o_ref[0, 0] = o.astype(o_ref.dtype)


def _diff_prompt(qt, k, vt, slopes2, lam_p, subln_g, lam_init):
    b, _, _, t = qt.shape
    tq = DIFF_QROWS
    tk = DIFF_KROWS
    assert t % tq == 0 and tq % tk == 0 and tk % CHUNK == 0
    vrows = HEAD_DIM + ONES_ROWS
    return pl.pallas_call(
        functools.partial(_diff_prompt_kernel, tq=tq, tk=tk, lam_init=lam_init),
        grid=(b, N_HEADS, t // tq),
        in_specs=[pl.BlockSpec(memory_space=pltpu.SMEM),
                  pl.BlockSpec((1, 4, DIFF_DIM), lambda bi, hi, qi: (0, 0, 0)),
                  pl.BlockSpec((HEAD_DIM, 1), lambda bi, hi, qi: (0, 0)),
                  pl.BlockSpec((1, 1, HEAD_DIM, tq), lambda bi, hi, qi: (bi, hi, 0, qi)),
                  pl.BlockSpec((1, t, HEAD_DIM), lambda bi, hi, qi: (bi, 0, hi)),
                  pl.BlockSpec((1, 1, HEAD_DIM, t), lambda bi, hi, qi: (bi, hi, 0, 0))],
        out_specs=pl.BlockSpec((1, 1, HEAD_DIM, tq), lambda bi, hi, qi: (bi, hi, 0, qi)),
        out_shape=jax.ShapeDtypeStruct((b, N_HEADS, HEAD_DIM, t), BF16),
        scratch_shapes=[pltpu.VMEM((vrows, 2 * tq), F32), pltpu.VMEM((1, HEAD_DIM), F32)],
        compiler_params=_params("arbitrary", "arbitrary", "arbitrary"),
        name="diff_prompt",
    )(slopes2, lam_p.reshape(1, 4, DIFF_DIM), subln_g.reshape(HEAD_DIM, 1), qt, k, vt)


def _softmax_parts(scores, values):
    m = functools.reduce(jnp.maximum, [jnp.max(s, axis=-1, keepdims=True) for s in scores])
    ps = [jnp.exp(s - m) for s in scores]
    l = sum(jnp.sum(p, axis=-1, keepdims=True) for p in ps)
    o = sum(_dot(p.astype(BF16), v) for p, v in zip(ps, values))
    return o, l


def _diff_sample_kernel(slope_ref, lam_ref, g_ref, q_ref, kn_ref, vn_ref, kc_ref, vc_ref, o_ref, *,
                        past, lam_init):
    h = pl.program_id(1)
    slope = slope_ref[h]
    q1, q2 = _split_q(q_ref[0])
    n = q1.shape[0]
    kc = kc_ref[0, 0].astype(BF16)
    vc = vc_ref[0, 0].astype(BF16)
    kn = kn_ref[0]
    vn = vn_ref[0]

    def bias(k_start, k_len):
        q_pos = past + _iota2((n, k_len), 0)
        k_pos = k_start + _iota2((n, k_len), 1)
        allowed = (k_pos // CHUNK) <= (q_pos // CHUNK)
        return jnp.where(allowed, -slope * jnp.abs(q_pos - k_pos).astype(F32), NEG_BIG)

    bias_c = bias(0, kc.shape[0])
    bias_n = bias(past, n)
    o1, l1 = _softmax_parts([_dot_nt(q1, kc) + bias_c, _dot_nt(q1, kn) + bias_n], [vc, vn])
    o2, l2 = _softmax_parts([_dot_nt(q2, kc) + bias_c, _dot_nt(q2, kn) + bias_n], [vc, vn])
    lam = _diff_lambda(lam_ref, lam_init)
    o_ref[0] = _diff_finish(o1, l1, o2, l2, lam, g_ref[...], lam_init).astype(o_ref.dtype)


def _sample_specs(n, past, layer):
    new_rows = lambda bi, hi: (bi, 0, hi)
    cache_rows = lambda bi, hi: (layer, bi, 0, hi)
    new = pl.BlockSpec((1, n, HEAD_DIM), new_rows)
    cache = pl.BlockSpec((1, 1, past, HEAD_DIM), cache_rows)
    return new, cache


def _diff_sample(q, k_new, v_new, cache_k, cache_v, layer, slopes, lam_p, subln_g, lam_init):
    b, n, e = q.shape
    past = cache_k.shape[2]
    new, cache = _sample_specs(n, past, layer)
    return pl.pallas_call(
        functools.partial(_diff_sample_kernel, past=past, lam_init=lam_init),
        grid=(b, N_HEADS),
        in_specs=[pl.BlockSpec(memory_space=pltpu.SMEM),
                  pl.BlockSpec((1, 4, DIFF_DIM), lambda bi, hi: (0, 0, 0)),
                  pl.BlockSpec((1, HEAD_DIM), lambda bi, hi: (0, 0)),
                  new, new, new, cache, cache],
        out_specs=new,
        out_shape=jax.ShapeDtypeStruct((b, n, e), BF16),
        compiler_params=_params("arbitrary", "arbitrary"),
        name="diff_sample",
    )(slopes, lam_p.reshape(1, 4, DIFF_DIM), subln_g.reshape(1, HEAD_DIM), q, k_new, v_new, cache_k, cache_v)


def _band_prompt_kernel(bias_ref, q_ref, k_ref, v_ref, o_ref, *, tq, span):
    q0 = pl.program_id(2) * tq
    back = jnp.minimum(q0, BAND_ROWS)
    start = pl.multiple_of(q0 - back, tq)
    shift = pl.multiple_of(BAND_ROWS - back, tq)
    k = k_ref[0, pl.ds(start, span), :]
    v = v_ref[0, pl.ds(start, span), :]
    s = _dot_nt(q_ref[0], k) + bias_ref[0, :, pl.ds(shift, span)]
    o, l = _softmax_parts([s], [v])
    o_ref[0] = (o / l).astype(o_ref.dtype)


def _band_prompt(q, k, v, bias):
    b, t, e = q.shape
    tq = bias.shape[1]
    wide = bias.shape[2]
    span = wide - BAND_ROWS
    assert t % tq == 0 and t >= span and BAND_ROWS % tq == 0
    head_rows = lambda bi, hi, qi: (bi, 0, hi)
    q_rows = lambda bi, hi, qi: (bi, qi, hi)
    return pl.pallas_call(
        functools.partial(_band_prompt_kernel, tq=tq, span=span),
        grid=(b, N_HEADS, t // tq),
        in_specs=[pl.BlockSpec((1, tq, wide), lambda bi, hi, qi: (hi, 0, 0)),
                  pl.BlockSpec((1, tq, HEAD_DIM), q_rows),
                  pl.BlockSpec((1, t, HEAD_DIM), head_rows),
                  pl.BlockSpec((1, t, HEAD_DIM), head_rows)],
        out_specs=pl.BlockSpec((1, tq, HEAD_DIM), q_rows),
        out_shape=jax.ShapeDtypeStruct((b, t, e), BF16),
        compiler_params=_params("arbitrary", "arbitrary", "arbitrary"),
        name="band_prompt",
    )(bias, q, k, v)


def _band_prompt_bias(rel_bias, tq):
    wide = 2 * BAND_ROWS + tq
    c0 = wide - 1 - BAND_ROWS
    n = tq + wide - 1
    lead = c0 - REL_CLIP
    rb = rel_bias.astype(F32)
    h = rb.shape[0]
    v = jnp.concatenate([jnp.broadcast_to(rb[:, :1], (h, lead)), rb,
                         jnp.broadcast_to(rb[:, -1:], (h, n - lead - rb.shape[1]))], axis=1)
    x = jnp.tile(v, (1, tq + 1))[:, :tq * (n + 1)].reshape(h, tq, n + 1)[:, :, :wide]
    tile = x[:, :, ::-1]
    i = jnp.arange(tq, dtype=jnp.int32)[:, None] // CHUNK
    kc = jnp.arange(wide, dtype=jnp.int32)[None, :] // CHUNK - BAND_CHUNKS
    in_band = (kc <= i) & (kc >= i - BAND_CHUNKS)
    return jnp.where(in_band[None], tile, NEG_BIG)


def _band_sample_kernel(bc_ref, bn_ref, q_ref, kn_ref, vn_ref, kc_ref, vc_ref, o_ref, *, past):
    q = q_ref[0]
    n = q.shape[0]
    kc = kc_ref[0, 0].astype(BF16)
    vc = vc_ref[0, 0].astype(BF16)
    keep = kc.shape[0]

    def valid(k_start, k_len):
        q_pos = past + _iota2((n, k_len), 0)
        k_pos = k_start + _iota2((n, k_len), 1)
        qc = q_pos // CHUNK
        kc_ = k_pos // CHUNK
        return (k_pos >= 0) & (kc_ <= qc) & (kc_ >= qc - BAND_CHUNKS)

    s_c = jnp.where(valid(past - keep, keep), _dot_nt(q, kc) + bc_ref[0], NEG_BIG)
    s_n = jnp.where(valid(past, n), _dot_nt(q, kn_ref[0]) + bn_ref[0], NEG_BIG)
    o, l = _softmax_parts([s_c, s_n], [vc, vn_ref[0]])
    o_ref[0] = (o / l).astype(o_ref.dtype)


def _band_sample(q, k_new, v_new, cache_k, cache_v, layer, bias_c, bias_n, past):
    b, n, e = q.shape
    keep = cache_k.shape[2]
    new, cache = _sample_specs(n, keep, layer)
    return pl.pallas_call(
        functools.partial(_band_sample_kernel, past=past),
        grid=(b, N_HEADS),
        in_specs=[pl.BlockSpec((1, n, keep), lambda bi, hi: (hi, 0, 0)),
                  pl.BlockSpec((1, n, n), lambda bi, hi: (hi, 0, 0)),
                  new, new, new, cache, cache],
        out_specs=new,
        out_shape=jax.ShapeDtypeStruct((b, n, e), BF16),
        compiler_params=_params("arbitrary", "arbitrary"),
        name="band_sample",
    )(bias_c, bias_n, q, k_new, v_new, cache_k, cache_v)


def _band_bias(rel_bias, q_pos, k_pos):
    idx = jnp.clip(q_pos[:, None] - k_pos[None, :], -REL_CLIP, REL_CLIP) + REL_CLIP
    return rel_bias.astype(F32)[:, idx]


def _strict_upper(nk):
    return jnp.where(_iota2((nk, nk), 0) > _iota2((nk, nk), 1), 1.0, 0.0).astype(BF16)


def _stick_block(q, k, v, upper, carry, valid):
    z = _dot_nt(q, k)
    soft = jnp.log2(1.0 + jnp.exp2(-jnp.abs(z)))
    log_beta = jnp.minimum(z, 0.0) - soft
    log_keep = -jnp.maximum(z, 0.0) - soft
    if valid is not None:
        log_keep = jnp.where(valid, log_keep, 0.0)
    hi = log_keep.astype(BF16)
    lo = (log_keep - hi.astype(F32)).astype(BF16)
    after = _dot(hi, upper) + _dot(lo, upper) + carry
    w = jnp.exp2(log_beta + after)
    if valid is not None:
        w = jnp.where(valid, w, 0.0)
    out = _dot(w.astype(BF16), v)
    return out, carry + jnp.sum(log_keep, axis=-1, keepdims=True)


def _stick_prompt_kernel(q_ref, k_ref, v_ref, o_ref, acc_ref, carry_ref, *, tq, tk):
    qb = pl.program_id(2)
    upper = _strict_upper(tk)
    q0 = pl.multiple_of(qb * tq, tq)
    acc_ref[...] = jnp.zeros_like(acc_ref)
    carry_ref[...] = jnp.zeros_like(carry_ref)

    def fold(k_start, r0, valid):
        out, carry = _stick_block(q_ref[0, r0:, :], k_ref[0, pl.ds(k_start, tk), :], v_ref[0, pl.ds(k_start, tk), :],
                                  upper, carry_ref[r0:, :], valid)
        acc_ref[r0:, :] += out
        carry_ref[r0:, :] = carry
        return carry

    for r0 in reversed(range(0, tq, tk)):
        rows = tq - r0
        fold(q0 + r0, r0, _iota2((rows, tk), 1) < _iota2((rows, tk), 0))

    def cond(state):
        j, alive = state
        return (j >= 0) & (alive > STICK_DEAD)

    def body(state):
        j, _ = state
        carry = fold(pl.multiple_of(j * tk, tk), 0, None)
        return j - 1, jnp.max(carry)

    lax.while_loop(cond, body, (qb * (tq // tk) - 1, jnp.max(carry_ref[...])))
    o_ref[0] = acc_ref[...].astype(o_ref.dtype)


def _stick_prompt(q, k, v):
    b, t, e = q.shape
    tq = STICK_QROWS
    tk = STICK_KROWS
    assert t % tq == 0 and tq % tk == 0
    head_rows = lambda bi, hi, qi: (bi, 0, hi)
    q_rows = lambda bi, hi, qi: (bi, qi, hi)
    return pl.pallas_call(
        functools.partial(_stick_prompt_kernel, tq=tq, tk=tk),
        grid=(b, N_HEADS, t // tq),
        in_specs=[pl.BlockSpec((1, tq, HEAD_DIM), q_rows),
                  pl.BlockSpec((1, t, HEAD_DIM), head_rows),
                  pl.BlockSpec((1, t, HEAD_DIM), head_rows)],
        out_specs=pl.BlockSpec((1, tq, HEAD_DIM), q_rows),
        out_shape=jax.ShapeDtypeStruct((b, t, e), BF16),
        scratch_shapes=[pltpu.VMEM((tq, HEAD_DIM), F32), pltpu.VMEM((tq, 1), F32)],
        compiler_params=_params("arbitrary", "arbitrary", "arbitrary"),
        name="stick_prompt",
    )(q, k, v)


def _stick_sample_kernel(q_ref, kn_ref, vn_ref, kc_ref, vc_ref, o_ref, *, blk):
    q = q_ref[0]
    n = q.shape[0]
    past = kc_ref.shape[2]
    upper = _strict_upper(blk)
    causal = _iota2((n, n), 1) < _iota2((n, n), 0)
    acc, carry = _stick_block(q, kn_ref[0], vn_ref[0], upper[:n, :n], jnp.zeros((n, 1), F32), causal)
    for j in reversed(range(past // blk)):
        kc = kc_ref[0, 0, j * blk:(j + 1) * blk, :].astype(BF16)
        vc = vc_ref[0, 0, j * blk:(j + 1) * blk, :].astype(BF16)
        out, carry = _stick_block(q, kc, vc, upper, carry, None)
        acc = acc + out
    o_ref[0] = acc.astype(o_ref.dtype)


def _stick_sample(q, k_new, v_new, cache_k, cache_v, layer):
    b, n, e = q.shape
    past = cache_k.shape[2]
    blk = min(256, past)
    assert past % blk == 0
    new, cache = _sample_specs(n, past, layer)
    return pl.pallas_call(
        functools.partial(_stick_sample_kernel, blk=blk),
        grid=(b, N_HEADS),
        in_specs=[new, new, new, cache, cache],
        out_specs=new,
        out_shape=jax.ShapeDtypeStruct((b, n, e), BF16),
        compiler_params=_params("arbitrary", "arbitrary"),
        name="stick_sample",
    )(q, k_new, v_new, cache_k, cache_v)


def kernel(x_prompt, x_sample, cache_a_k, cache_a_v, cache_b_k, cache_b_v, cache_c_k, cache_c_v,
           norm_g, w_in, w_out, a_lambda, a_subln_g, b_rel_bias, final_g):
    bp, t, d = x_prompt.shape
    bs, n, _ = x_sample.shape
    depth = w_in.shape[0]
    e = w_out.shape[1]
    past = cache_a_k.shape[2]
    keep_s = cache_b_k.shape[2]
    keep_p = min(BAND_ROWS, t)
    heads5 = lambda a, b_: a.reshape(b_, -1, N_HEADS, HEAD_DIM)
    flat_cache = lambda c: c.reshape(c.shape[0], c.shape[1], c.shape[2], e)

    slopes = 2.0 ** (-8.0 * jnp.arange(1, N_HEADS + 1, dtype=F32) / N_HEADS)
    w_in_bf = w_in.astype(BF16)
    w_out_bf = w_out.astype(BF16)
    caches = [(flat_cache(cache_a_k), flat_cache(cache_a_v)),
              (flat_cache(cache_b_k), flat_cache(cache_b_v)),
              (flat_cache(cache_c_k), flat_cache(cache_c_v))]

    xp = x_prompt.reshape(bp * t, d)
    xs = x_sample.reshape(bs * n, d)
    new_p = [([], []) for _ in range(N_MIXERS)]
    new_s = [([], []) for _ in range(N_MIXERS)]
    for i in range(depth):
        kind = i % N_MIXERS
        j = i // N_MIXERS
        last = i == depth - 1
        qscale = DIFF_DIM ** -0.5 if kind == 0 else HEAD_DIM ** -0.5
        qscale_p = qscale * LOG2E if kind != 1 else qscale
        qscale_s = qscale * LOG2E if kind == 2 else qscale
        qp, kp, vp, kpb, vpb, sgp = _in_proj(xp, norm_g[i], w_in_bf[i], qscale_p)
        qs, ks, vs, ksb, vsb, sgs = _in_proj(xs, norm_g[i], w_in_bf[i], qscale_s)
        seq_p = lambda a: a.reshape(bp, t, e)
        seq_s = lambda a: a.reshape(bs, n, e)
        cache_k, cache_v = caches[kind]
        if kind == 0:
            lam_init = 0.8 - 0.6 * math.exp(-0.3 * i)
            head_major = lambda a: heads5(a, bp).transpose(0, 2, 3, 1)
            op_t = _diff_prompt(head_major(qp), seq_p(kpb), head_major(vpb), slopes * LOG2E, a_lambda[j],
                                a_subln_g[j], lam_init)
            op = op_t.transpose(0, 3, 1, 2)
            os_ = _diff_sample(seq_s(qs), seq_s(ksb), seq_s(vsb), cache_k, cache_v, j, slopes,
                               a_lambda[j], a_subln_g[j], lam_init)
            new_p[0][0].append(heads5(kp, bp))
            new_p[0][1].append(heads5(vp, bp))
            new_s[0][0].append(heads5(ks, bs))
            new_s[0][1].append(heads5(vs, bs))
        elif kind == 1:
            op = _band_prompt(seq_p(qp), seq_p(kpb), seq_p(vpb), _band_prompt_bias(b_rel_bias[j], BAND_QROWS))
            pos_s = past + jnp.arange(n, dtype=jnp.int32)
            bias_c = _band_bias(b_rel_bias[j], pos_s, past - keep_s + jnp.arange(keep_s, dtype=jnp.int32))
            bias_n = _band_bias(b_rel_bias[j], pos_s, pos_s)
            os_ = _band_sample(seq_s(qs), seq_s(ksb), seq_s(vsb), cache_k, cache_v, j, bias_c, bias_n, past)
            new_p[1][0].append(heads5(kp, bp)[:, t - keep_p:])
            new_p[1][1].append(heads5(vp, bp)[:, t - keep_p:])
            new_s[1][0].append(jnp.concatenate([cache_b_k[j], heads5(ks, bs)], axis=1)[:, n:])
            new_s[1][1].append(jnp.concatenate([cache_b_v[j], heads5(vs, bs)], axis=1)[:, n:])
        else:
            op = _stick_prompt(seq_p(qp), seq_p(kpb), seq_p(vpb))
            os_ = _stick_sample(seq_s(qs), seq_s(ksb), seq_s(vsb), cache_k, cache_v, j)
            new_p[2][0].append(heads5(kp, bp))
            new_p[2][1].append(heads5(vp, bp))
            new_s[2][0].append(heads5(ks, bs))
            new_s[2][1].append(heads5(vs, bs))
        xp = _out_proj(op.reshape(bp * t, e), sgp, w_out_bf[i], xp, final_g, last)
        xs = _out_proj(os_.reshape(bs * n, e), sgs, w_out_bf[i], xs, final_g, last)

    outs = [xp.reshape(bp, t, d), xs.reshape(bs, n, d)]
    for group in (new_p, new_s):
        for ks_list, vs_list in group:
            outs.append(jnp.stack(ks_list))
            outs.append(jnp.stack(vs_list))
    return tuple(outs)
```
